```python
import jax
import jax.numpy as jnp
from jax import lax
import numpy as np

D_MODEL = 2048
BATCH = 2
SEQ = 4096
DEPTH = 2
DEC_BATCH = 8
DEC_SEQ = 4
PAST_LEN = 16384
PAGE_SIZE = 128

N_MIXERS = 2
EXPAND = 2
BRANCH = EXPAND * D_MODEL
N_HEADS = 16
HEAD_K = D_MODEL // N_HEADS
HEAD_V = BRANCH // N_HEADS
QK_WIDTH = N_HEADS * HEAD_K
IN_WIDTH = 2 * QK_WIDTH + 2 * BRANCH
N_A_LAYERS = (DEPTH + 1) // N_MIXERS
N_B_LAYERS = DEPTH // N_MIXERS
CHUNK_A = 64
Q_BLOCK = 128
SB_BIAS_INIT = -10.0
ALPHA = (2.0 * DEPTH) ** 0.25
BETA = (8.0 * DEPTH) ** -0.25
LN_EPS = 1e-5
RMS_EPS = 1e-6

kernel_name = 'hgrn2_stickbreaking_hybrid_step'


def _layer_norm(x, gain, bias):
    xf = x.astype(jnp.float32)
    mu = jnp.mean(xf, axis=-1, keepdims=True)
    var = jnp.mean(jnp.square(xf - mu), axis=-1, keepdims=True)
    return ((xf - mu) * lax.rsqrt(var + LN_EPS) * gain + bias).astype(x.dtype)


def _modulate(x, c, w_ada_l, b_ada_l):
    shift, scale, gate = jnp.split(c @ w_ada_l + b_ada_l, 3, axis=-1)
    return x * (1 + scale[:, None, :]) + shift[:, None, :], gate


def _residual_out(x, branch, gate, w_out_l, gain, bias):
    return _layer_norm(ALPHA * x + gate[:, None, :] * (branch @ w_out_l), gain, bias)


def _split_in(u):
    b, t, _ = u.shape
    q = u[..., :QK_WIDTH].reshape(b, t, N_HEADS, HEAD_K)
    k = u[..., QK_WIDTH:2 * QK_WIDTH].reshape(b, t, N_HEADS, HEAD_K)
    v = u[..., 2 * QK_WIDTH:2 * QK_WIDTH + BRANCH].reshape(b, t, N_HEADS, HEAD_V)
    g = u[..., 2 * QK_WIDTH + BRANCH:]
    return q, k, v, g


def _gla_chunked(q, k, v, log_f, s0):
    b, t, h, dk = q.shape
    dv = v.shape[-1]
    c = min(CHUNK_A, t)
    pad = (-t) % c
    if pad:
        pw = ((0, 0), (0, pad), (0, 0), (0, 0))
        q, k, v, log_f = [jnp.pad(a, pw) for a in (q, k, v, log_f)]
    n = (t + pad) // c

    def chunks(a):
        return a.reshape(b, n, c, h, a.shape[-1]).transpose(1, 0, 3, 2, 4)

    causal = jnp.tril(jnp.ones((c, c), dtype=bool))

    def step(s, xs):
        qb, kb, vb, gb = xs
        cum = jnp.cumsum(gb, axis=-2)
        ref = cum[:, :, c // 2:c // 2 + 1, :]
        last = cum[:, :, -1:, :]
        scores = jnp.einsum('bhtk,bhsk->bhts', qb * jnp.exp(cum - ref), kb * jnp.exp(ref - cum))
        scores = jnp.where(causal, scores, 0.0)
        o = (jnp.einsum('bhts,bhsv->bhtv', scores, vb)
             + jnp.einsum('bhtk,bhkv->bhtv', qb * jnp.exp(cum), s))
        s = (jnp.exp(last[:, :, 0, :])[..., None] * s
             + jnp.einsum('bhsk,bhsv->bhkv', kb * jnp.exp(last - cum), vb))
        return s, o

    s, o = lax.scan(step, s0, (chunks(q), chunks(k), chunks(v), chunks(log_f)))
    o = o.transpose(1, 0, 3, 2, 4).reshape(b, n * c, h, dv)[:, :t]
    return o, s


def _hgrn_mixer(u, lb, norm_gain, s0):
    qp, fp, ip, g = _split_in(u)
    q = jax.nn.silu(qp.astype(jnp.float32))
    zf = fp.astype(jnp.float32)
    lb = lb.reshape(N_HEADS, HEAD_K)
    log_f = jnp.log(lb + (1 - lb) * jax.nn.sigmoid(zf))
    k = (1 - lb) * jax.nn.sigmoid(-zf)
    o, s = _gla_chunked(q, k, ip.astype(jnp.float32), log_f, s0.astype(jnp.float32))
    o = o * lax.rsqrt(jnp.mean(jnp.square(o), axis=-1, keepdims=True) + RMS_EPS) * norm_gain.astype(jnp.float32)
    b, t = u.shape[:2]
    return (o.reshape(b, t, BRANCH) * jax.nn.silu(g.astype(jnp.float32))).astype(u.dtype), s


def _stick_breaking_weights(z, mask, carry):
    log_keep = jax.nn.log_sigmoid(-z)
    if mask is not None:
        log_keep = jnp.where(mask, log_keep, 0.0)
    later = lax.cumsum(log_keep, axis=z.ndim - 1, reverse=True) - log_keep
    log_w = jax.nn.log_sigmoid(z) + later
    if carry is not None:
        log_w = log_w + carry[..., None]
    w = jnp.exp(log_w)
    if mask is not None:
        w = jnp.where(mask, w, 0.0)
    return w, log_keep


def _sb_prompt(q, k, v, bias):
    b, t, h, dk = q.shape
    qh = (q.astype(jnp.float32) * dk ** -0.5).transpose(0, 2, 1, 3)
    kh = k.astype(jnp.float32).transpose(0, 2, 1, 3)
    vh = v.astype(jnp.float32).transpose(0, 2, 1, 3)
    bias = bias.astype(jnp.float32)[None, :, None, None]
    key_pos = jnp.arange(t)

    def block(i):
        start = i * Q_BLOCK
        qb = lax.dynamic_slice_in_dim(qh, start, Q_BLOCK, axis=2)
        z = jnp.einsum('bhqd,bhsd->bhqs', qb, kh) + bias
        mask = key_pos[None, :] < (start + jnp.arange(Q_BLOCK))[:, None]
        w, _ = _stick_breaking_weights(z, mask, None)
        return jnp.einsum('bhqs,bhsv->bhqv', w, vh)

    o = lax.map(block, jnp.arange(t // Q_BLOCK))
    return o.transpose(1, 0, 3, 2, 4).reshape(b, t, h, vh.shape[-1])


def _sb_sample(q, k, v, cache_k, cache_v, page_table, slot, bias):
    b, t, h, dk = q.shape
    qf = q.astype(jnp.float32) * dk ** -0.5
    bias = bias.astype(jnp.float32)[None, :, None, None]
    z = jnp.einsum('bthd,bshd->bhts', qf, k.astype(jnp.float32)) + bias
    pos = jnp.arange(t)
    w, log_keep = _stick_breaking_weights(z, pos[None, :] < pos[:, None], None)
    o = jnp.einsum('bhts,bshv->bthv', w, v.astype(jnp.float32))
    surv0 = jnp.sum(log_keep, axis=-1)

    def page_step(state, p):
        surv, acc = state
        phys = page_table[:, p]
        kp = cache_k[phys, :, slot].astype(jnp.float32)
        vp = cache_v[phys, :, slot].astype(jnp.float32)
        zp = jnp.einsum('bthd,bshd->bhts', qf, kp) + bias
        wp, lp = _stick_breaking_weights(zp, None, surv)
        acc = acc + jnp.einsum('bhts,bshv->bthv', wp, vp)
        return (surv + jnp.sum(lp, axis=-1), acc), None

    n_pages = page_table.shape[1]
    (_, o), _ = lax.scan(page_step, (surv0, o), jnp.arange(n_pages - 1, -1, -1))
    return o


def setup_inputs(seed: int = 0) -> dict:
    key = jax.random.key(seed)
    ks = jax.random.split(key, 17)
    nrm = jax.random.normal
    f32 = jnp.float32
    n_pages = PAST_LEN // PAGE_SIZE
    n_used = DEC_BATCH * n_pages
    n_pool = n_used + (n_used + 3) // 4
    x_prompt = nrm(ks[0], (BATCH, SEQ, D_MODEL), f32)
    x_sample = nrm(ks[1], (DEC_BATCH, DEC_SEQ, D_MODEL), f32)
    state_hgrn = 0.5 * nrm(ks[2], (N_A_LAYERS, DEC_BATCH, N_HEADS, HEAD_K, HEAD_V), f32)
    cache_k = nrm(ks[3], (n_pool, PAGE_SIZE, N_B_LAYERS, N_HEADS, HEAD_K), f32)
    cache_v = 0.5 * nrm(ks[4], (n_pool, PAGE_SIZE, N_B_LAYERS, N_HEADS, HEAD_V), f32)
    page_table = jax.random.permutation(ks[5], n_pool)[:n_used].reshape(DEC_BATCH, n_pages).astype(jnp.int32)
    c_prompt = nrm(ks[6], (BATCH, D_MODEL), f32)
    c_sample = nrm(ks[7], (DEC_BATCH, D_MODEL), f32)
    w_ada = (0.5 * D_MODEL ** -0.5) * nrm(ks[8], (DEPTH, D_MODEL, 3 * D_MODEL), f32)
    b_ada = 0.01 * nrm(ks[9], (DEPTH, 3 * D_MODEL), f32)
    col_scale = jnp.concatenate([jnp.ones((2 * QK_WIDTH,), f32), jnp.full((BRANCH,), BETA, f32),
                                 jnp.ones((BRANCH,), f32)])
    w_in = nrm(ks[10], (DEPTH, D_MODEL, IN_WIDTH), f32) * (D_MODEL ** -0.5) * col_scale
    w_out = nrm(ks[11], (DEPTH, BRANCH, D_MODEL), f32) * (BETA * BRANCH ** -0.5)
    hgrn_lb_logits = 0.1 * nrm(ks[12], (N_A_LAYERS + 1, QK_WIDTH), f32)
    hgrn_norm_gain = 1.0 + 0.1 * nrm(ks[13], (N_A_LAYERS, HEAD_V), f32)
    sb_logit_bias = SB_BIAS_INIT + 0.5 * nrm(ks[16], (N_B_LAYERS, N_HEADS), f32)
    ln_gain = 1.0 + 0.1 * nrm(ks[14], (DEPTH, D_MODEL), f32)
    ln_bias = 0.01 * nrm(ks[15], (DEPTH, D_MODEL), f32)
    return {'x_prompt': x_prompt, 'x_sample': x_sample, 'state_hgrn': state_hgrn,
            'cache_k': cache_k, 'cache_v': cache_v, 'page_table': page_table,
            'c_prompt': c_prompt, 'c_sample': c_sample, 'w_ada': w_ada, 'b_ada': b_ada,
            'w_in': w_in, 'w_out': w_out, 'hgrn_lb_logits': hgrn_lb_logits,
            'hgrn_norm_gain': hgrn_norm_gain, 'sb_logit_bias': sb_logit_bias,
            'ln_gain': ln_gain, 'ln_bias': ln_bias}


def reference(x_prompt, x_sample, state_hgrn, cache_k, cache_v, page_table, c_prompt, c_sample,
              w_ada, b_ada, w_in, w_out, hgrn_lb_logits, hgrn_norm_gain, sb_logit_bias, ln_gain, ln_bias):
    lower_bounds = jnp.cumsum(jax.nn.softmax(hgrn_lb_logits.astype(jnp.float32), axis=0), axis=0)
    xp, xs = x_prompt, x_sample
    hgrn_p, hgrn_s, kp_rows, vp_rows, ks_rows, vs_rows = [], [], [], [], [], []
    for layer in range(DEPTH):
        slot = layer // N_MIXERS
        hp, gate_p = _modulate(xp, c_prompt, w_ada[layer], b_ada[layer])
        hs, gate_s = _modulate(xs, c_sample, w_ada[layer], b_ada[layer])
        up = hp @ w_in[layer]
        us = hs @ w_in[layer]
        if layer % N_MIXERS == 0:
            s_zero = jnp.zeros((xp.shape[0], N_HEADS, HEAD_K, HEAD_V), jnp.float32)
            bp, sp = _hgrn_mixer(up, lower_bounds[slot], hgrn_norm_gain[slot], s_zero)
            bs, ss = _hgrn_mixer(us, lower_bounds[slot], hgrn_norm_gain[slot], state_hgrn[slot])
            hgrn_p.append(sp.astype(state_hgrn.dtype))
            hgrn_s.append(ss.astype(state_hgrn.dtype))
        else:
            qp, kp, vp, gp = _split_in(up)
            qs, ks, vs, gs = _split_in(us)
            op = _sb_prompt(qp, kp, vp, sb_logit_bias[slot])
            os_ = _sb_sample(qs, ks, vs, cache_k, cache_v, page_table, slot, sb_logit_bias[slot])
            bp = (op.reshape(up.shape[0], up.shape[1], BRANCH) * jax.nn.silu(gp.astype(jnp.float32))).astype(up.dtype)
            bs = (os_.reshape(us.shape[0], us.shape[1], BRANCH) * jax.nn.silu(gs.astype(jnp.float32))).astype(us.dtype)
            kp_rows.append(kp.astype(cache_k.dtype))
            vp_rows.append(vp.astype(cache_v.dtype))
            ks_rows.append(ks.astype(cache_k.dtype))
            vs_rows.append(vs.astype(cache_v.dtype))
        xp = _residual_out(xp, bp, gate_p, w_out[layer], ln_gain[layer], ln_bias[layer])
        xs = _residual_out(xs, bs, gate_s, w_out[layer], ln_gain[layer], ln_bias[layer])
    y_prompt = xp
    y_sample = xs
    state_hgrn_prompt = jnp.stack(hgrn_p, axis=0)
    state_hgrn_sample = jnp.stack(hgrn_s, axis=0)
    k_rows_prompt = jnp.stack(kp_rows, axis=2)
    v_rows_prompt = jnp.stack(vp_rows, axis=2)
    k_rows_sample = jnp.stack(ks_rows, axis=2)
    v_rows_sample = jnp.stack(vs_rows, axis=2)
    return (y_prompt, y_sample, state_hgrn_prompt, state_hgrn_sample, k_rows_prompt, v_rows_prompt, k_rows_sample, v_rows_sample)
```

```python
import functools

import jax
import jax.numpy as jnp
from jax import lax
from jax.experimental import pallas as pl
from jax.experimental.pallas import tpu as pltpu

F32 = jnp.float32
BF16 = jnp.bfloat16

LN_EPS = 1e-5
RMS_EPS = 1e-6
HGRN_CHUNK = 64
N_MIXERS = 2

NT_DIMS = (((1,), (1,)), ((), ()))
TN_DIMS = (((0,), (0,)), ((), ()))


def _vmem(mib):
    return mib * 1024 * 1024


def _sigmoid(x):
    return 1.0 / (1.0 + jnp.exp(-x))


def _silu(x):
    return x * _sigmoid(x)


def _neg_softplus(z):
    return -(jnp.maximum(z, 0.0) + jnp.log1p(jnp.exp(-jnp.abs(z))))


def _split2(x):
    hi = x.astype(BF16)
    lo = (x - hi.astype(F32)).astype(BF16)
    return hi, lo


def _split3(x):
    hi = x.astype(BF16)
    r = x - hi.astype(F32)
    mid = r.astype(BF16)
    lo = (r - mid.astype(F32)).astype(BF16)
    return hi, mid, lo


def _ada_kernel(c_ref, w_ref, b_ref, o_ref):
    o_ref[...] = jnp.dot(c_ref[...], w_ref[...], precision=lax.Precision.HIGHEST,
                         preferred_element_type=F32) + b_ref[...]


def _ada_call(c_all, w_ada, b_ada):
    depth, d, n = w_ada.shape
    rows = c_all.shape[0]
    tn = 1024
    return pl.pallas_call(
        _ada_kernel,
        out_shape=jax.ShapeDtypeStruct((depth, rows, n), F32),
        grid=(depth, n // tn),
        in_specs=[
            pl.BlockSpec((rows, d), lambda l, j: (0, 0)),
            pl.BlockSpec((None, d, tn), lambda l, j: (l, 0, j)),
            pl.BlockSpec((None, 1, tn), lambda l, j: (l, 0, j)),
        ],
        out_specs=pl.BlockSpec((None, rows, tn), lambda l, j: (l, 0, j)),
        compiler_params=pltpu.CompilerParams(
            dimension_semantics=("arbitrary", "arbitrary"), vmem_limit_bytes=_vmem(40)),
        name="ada_mod",
    )(c_all, w_ada, b_ada.reshape(depth, 1, n))


def _inproj_kernel(x_ref, shift_ref, scale_ref, w_ref, u_ref, h_ref):
    @pl.when(pl.program_id(1) == 0)
    def _():
        h_ref[...] = (x_ref[...] * (1.0 + scale_ref[...]) + shift_ref[...]).astype(BF16)

    u_ref[...] = jnp.dot(h_ref[...], w_ref[...], preferred_element_type=F32)


def _inproj_call(x2, shift, scale, w_bf, layer, rows_per_mod, tm):
    m, d = x2.shape
    n = w_bf.shape[-1]
    tn = 1024
    r = shift.shape[1]
    tiles_per_mod = rows_per_mod // tm
    mod_spec = pl.BlockSpec((None, r, d), lambda i, j: (i // tiles_per_mod, 0, 0))
    return pl.pallas_call(
        _inproj_kernel,
        out_shape=jax.ShapeDtypeStruct((m, n), F32),
        grid=(m // tm, n // tn),
        in_specs=[
            pl.BlockSpec((tm, d), lambda i, j: (i, 0)),
            mod_spec, mod_spec,
            pl.BlockSpec((None, d, tn), lambda i, j: (layer, 0, j)),
        ],
        out_specs=pl.BlockSpec((tm, tn), lambda i, j: (i, j)),
        scratch_shapes=[pltpu.VMEM((tm, d), BF16)],
        compiler_params=pltpu.CompilerParams(
            dimension_semantics=("arbitrary", "arbitrary"), vmem_limit_bytes=_vmem(48)),
        name="in_proj",
    )(x2, shift, scale, w_bf)


def _outproj_kernel(x_ref, br_ref, w_ref, gate_ref, lng_ref, lnb_ref, y_ref, acc_ref, *, alpha, nk):
    k = pl.program_id(1)

    @pl.when(k == 0)
    def _():
        acc_ref[...] = jnp.zeros_like(acc_ref)

    acc_ref[...] += jnp.dot(br_ref[...], w_ref[...], preferred_element_type=F32)

    @pl.when(k == nk - 1)
    def _():
        v = alpha * x_ref[...] + gate_ref[...] * acc_ref[...]
        mu = jnp.mean(v, axis=-1, keepdims=True)
        dv = v - mu
        var = jnp.mean(dv * dv, axis=-1, keepdims=True)
        y_ref[...] = dv * lax.rsqrt(var + LN_EPS) * lng_ref[...] + lnb_ref[...]


def _outproj_call(x2, br2, w_bf, gate, ln_gain, ln_bias, layer, rows_per_mod, tm, alpha):
    m, d = x2.shape
    kdim = br2.shape[1]
    tk = 2048
    nk = kdim // tk
    r = gate.shape[1]
    tiles_per_mod = rows_per_mod // tm
    depth = ln_gain.shape[0]
    vec_spec = pl.BlockSpec((None, 1, d), lambda i, k: (layer, 0, 0))
    return pl.pallas_call(
        functools.partial(_outproj_kernel, alpha=alpha, nk=nk),
        out_shape=jax.ShapeDtypeStruct((m, d), F32),
        grid=(m // tm, nk),
        in_specs=[
            pl.BlockSpec((tm, d), lambda i, k: (i, 0)),
            pl.BlockSpec((tm, tk), lambda i, k: (i, k)),
            pl.BlockSpec((None, tk, d), lambda i, k: (layer, k, 0)),
            pl.BlockSpec((None, r, d), lambda i, k: (i // tiles_per_mod, 0, 0)),
            vec_spec, vec_spec,
        ],
        out_specs=pl.BlockSpec((tm, d), lambda i, k: (i, 0)),
        scratch_shapes=[pltpu.VMEM((tm, d), F32)],
        compiler_params=pltpu.CompilerParams(
            dimension_semantics=("arbitrary", "arbitrary"), vmem_limit_bytes=_vmem(48)),
        name="out_proj_ln",
    )(x2, br2, w_bf, gate, ln_gain.reshape(depth, 1, d), ln_bias.reshape(depth, 1, d))


def _hgrn_kernel(lbl_ref, gain_ref, q_ref, f_ref, i_ref, g_ref, s0_ref, o_ref, s_ref, st_ref, *,
                 slot, chunk, n_chunks, t_valid):
    tb = pl.program_id(2)
    c = chunk

    @pl.when(tb == 0)
    def _():
        st_ref[...] = s0_ref[...].T

    lg = lbl_ref[...]
    e = jnp.exp(lg - jnp.max(lg, axis=0, keepdims=True))
    lb = jnp.sum(e[:slot + 1], axis=0, keepdims=True) / jnp.sum(e, axis=0, keepdims=True)
    gain = gain_ref[...]

    row = lax.broadcasted_iota(jnp.int32, (c, c), 0)
    col = lax.broadcasted_iota(jnp.int32, (c, c), 1)
    causal = col <= row
    tri = jnp.where(causal, 1.0, 0.0).astype(BF16)

    for ci in range(n_chunks):
        sl = pl.ds(ci * c, c)
        zf = f_ref[sl, :]
        t = jnp.exp(-jnp.abs(zf))
        r = 1.0 / (1.0 + t)
        pos = zf >= 0.0
        sig = jnp.where(pos, r, t * r)
        nsig = jnp.where(pos, t * r, r)
        log_f = jnp.log(lb + (1.0 - lb) * sig)
        k = (1.0 - lb) * nsig
        if t_valid is not None:
            tpos = (tb * n_chunks + ci) * c + lax.broadcasted_iota(jnp.int32, (c, 1), 0)
            valid = tpos < t_valid
            log_f = jnp.where(valid, log_f, 0.0)
            k = jnp.where(valid, k, 0.0)
        h3 = _split3(log_f)
        cum = (jnp.dot(tri, h3[0], preferred_element_type=F32)
               + jnp.dot(tri, h3[1], preferred_element_type=F32)
               + jnp.dot(tri, h3[2], preferred_element_type=F32))
        ref = cum[c // 2:c // 2 + 1, :]
        last = cum[c - 1:c, :]
        q = _silu(q_ref[sl, :])
        qe = (q * jnp.exp(cum - ref)).astype(BF16)
        ke = (k * jnp.exp(ref - cum)).astype(BF16)
        qd = (q * jnp.exp(cum)).astype(BF16)
        kd = (k * jnp.exp(last - cum)).astype(BF16)
        vb = i_ref[sl, :].astype(BF16)
        scores = lax.dot_general(qe, ke, NT_DIMS, preferred_element_type=F32)
        scores = jnp.where(causal, scores, 0.0).astype(BF16)
        st = st_ref[...]
        o = (jnp.dot(scores, vb, preferred_element_type=F32)
             + lax.dot_general(qd, st.astype(BF16), NT_DIMS, preferred_element_type=F32))
        st_ref[...] = jnp.exp(last) * st + lax.dot_general(vb, kd, TN_DIMS, preferred_element_type=F32)
        ms = jnp.mean(o * o, axis=-1, keepdims=True)
        on = o * lax.rsqrt(ms + RMS_EPS) * gain
        o_ref[sl, :] = (on * _silu(g_ref[sl, :])).astype(o_ref.dtype)

    @pl.when(tb == pl.num_programs(2) - 1)
    def _():
        s_ref[...] = st_ref[...].T


def _hgrn_call(u3, lb_logits, norm_gain, s0, slot, heads, hk, hv, t_valid=None):
    b, t, _ = u3.shape
    tblk = min(t, 256)
    n_chunks = tblk // HGRN_CHUNK
    qk = heads * hk
    kv = qk // hv
    n_lb = lb_logits.shape[0]
    kern = functools.partial(_hgrn_kernel, slot=slot, chunk=HGRN_CHUNK, n_chunks=n_chunks,
                             t_valid=t_valid)
    return pl.pallas_call(
        kern,
        out_shape=(jax.ShapeDtypeStruct((b, t, heads * hv), BF16),
                   jax.ShapeDtypeStruct((b, heads, hk, hv), F32)),
        grid=(b, heads, t // tblk),
        in_specs=[
            pl.BlockSpec((n_lb, hk), lambda bi, h, tb: (0, h)),
            pl.BlockSpec((None, 1, hv), lambda bi, h, tb: (slot, 0, 0)),
            pl.BlockSpec((None, tblk, hk), lambda bi, h, tb: (bi, tb, h)),
            pl.BlockSpec((None, tblk, hk), lambda bi, h, tb: (bi, tb, heads + h)),
            pl.BlockSpec((None, tblk, hv), lambda bi, h, tb: (bi, tb, 2 * kv + h)),
            pl.BlockSpec((None, tblk, hv), lambda bi, h, tb: (bi, tb, 2 * kv + heads + h)),
            pl.BlockSpec((None, None, hk, hv), lambda bi, h, tb: (bi, h, 0, 0)),
        ],
        out_specs=(
            pl.BlockSpec((None, tblk, hv), lambda bi, h, tb: (bi, tb, h)),
            pl.BlockSpec((None, None, hk, hv), lambda bi, h, tb: (bi, h, 0, 0)),
        ),
        scratch_shapes=[pltpu.VMEM((hv, hk), F32)],
        compiler_params=pltpu.CompilerParams(
            dimension_semantics=("arbitrary", "arbitrary", "arbitrary"), vmem_limit_bytes=_vmem(32)),
        name="hgrn_mixer",
    )(lb_logits, norm_gain.reshape(-1, 1, hv), u3, u3, u3, u3, s0)


def _sbp_kernel(bias_ref, q_ref, k_ref, v_ref, g_ref, o_ref, acc_ref, carry_ref, *, tq, scale):
    h = pl.program_id(1)
    qi = pl.program_id(2)
    bias = bias_ref[h]
    q = (q_ref[...] * scale).astype(BF16)
    row = lax.broadcasted_iota(jnp.int32, (tq, tq), 0)
    col = lax.broadcasted_iota(jnp.int32, (tq, tq), 1)
    suffix = jnp.where(row >= col, 1.0, 0.0).astype(BF16)
    strict = col < row

    acc_ref[...] = jnp.zeros_like(acc_ref)
    carry_ref[...] = jnp.zeros_like(carry_ref)

    def block(j, masked):
        start = pl.multiple_of(j * tq, tq)
        kb = k_ref[pl.ds(start, tq), :].astype(BF16)
        vb = v_ref[pl.ds(start, tq), :].astype(BF16)
        z = lax.dot_general(q, kb, NT_DIMS, preferred_element_type=F32) + bias
        lk = _neg_softplus(z)
        if masked:
            lk = jnp.where(strict, lk, 0.0)
        hi, lo = _split2(lk)
        rs = (jnp.dot(hi, suffix, preferred_element_type=F32)
              + jnp.dot(lo, suffix, preferred_element_type=F32))
        w = jnp.exp(z + rs + carry_ref[...])
        if masked:
            w = jnp.where(strict, w, 0.0)
        acc_ref[...] += jnp.dot(w.astype(BF16), vb, preferred_element_type=F32)
        carry_ref[...] += rs[:, 0:1]

    block(qi, True)

    def body(jj, c):
        block(qi - 1 - jj, False)
        return c

    lax.fori_loop(0, qi, body, 0)
    o_ref[...] = (acc_ref[...] * _silu(g_ref[...])).astype(o_ref.dtype)


def _sbp_call(u3, bias, heads, hk, hv):
    b, t, _ = u3.shape
    tq = min(t, 256)
    qk = heads * hk
    kv = qk // hv
    kern = functools.partial(_sbp_kernel, tq=tq, scale=float(hk) ** -0.5)
    grid_spec = pltpu.PrefetchScalarGridSpec(
        num_scalar_prefetch=1,
        grid=(b, heads, t // tq),
        in_specs=[
            pl.BlockSpec((None, tq, hk), lambda bi, h, qi, bias: (bi, qi, h)),
            pl.BlockSpec((None, t, hk), lambda bi, h, qi, bias: (bi, 0, heads + h)),
            pl.BlockSpec((None, t, hv), lambda bi, h, qi, bias: (bi, 0, 2 * kv + h)),
            pl.BlockSpec((None, tq, hv), lambda bi, h, qi, bias: (bi, qi, 2 * kv + heads + h)),
        ],
        out_specs=pl.BlockSpec((None, tq, hv), lambda bi, h, qi, bias: (bi, qi, h)),
        scratch_shapes=[pltpu.VMEM((tq, hv), F32), pltpu.VMEM((tq, 1), F32)],
    )
    return pl.pallas_call(
        kern,
        out_shape=jax.ShapeDtypeStruct((b, t, heads * hv), BF16),
        grid_spec=grid_spec,
        compiler_params=pltpu.CompilerParams(
            dimension_semantics=("arbitrary", "arbitrary", "arbitrary"), vmem_limit_bytes=_vmem(40)),
        name="sb_prompt",
    )(bias, u3, u3, u3, u3)


def _sbs_kernel(pt_ref, bias_ref, wq_ref, kn_ref, vn_ref, *rest, pages_per_step, t_new, heads, hv, scale):
    p = pages_per_step
    k_refs = rest[:p]
    v_refs = rest[p:2 * p]
    g_ref, o_ref, acc_ref, carry_ref = rest[2 * p:]
    s = pl.program_id(1)
    n_keys = kn_ref.shape[0]
    n_cols = heads * t_new

    row = lax.broadcasted_iota(jnp.int32, (n_keys, n_keys), 0)
    col = lax.broadcasted_iota(jnp.int32, (n_keys, n_keys), 1)
    newer = jnp.where(col > row, 1.0, 0.0).astype(BF16)

    def page(kp, vp, mask):
        zt = (jnp.dot(kp.astype(BF16), wq_ref[...], preferred_element_type=F32) * scale
              + bias_ref[...])
        lk = _neg_softplus(zt)
        if mask is not None:
            lk = jnp.where(mask, lk, 0.0)
        hi, lo = _split2(lk)
        later = (jnp.dot(newer, hi, preferred_element_type=F32)
                 + jnp.dot(newer, lo, preferred_element_type=F32))
        w = jnp.exp(zt + lk + later + carry_ref[...])
        if mask is not None:
            w = jnp.where(mask, w, 0.0)
        acc_ref[...] += lax.dot_general(w.astype(BF16), vp.astype(BF16), TN_DIMS,
                                        preferred_element_type=F32)
        carry_ref[...] += later[0:1, :] + lk[0:1, :]

    @pl.when(s == 0)
    def _():
        acc_ref[...] = jnp.zeros_like(acc_ref)
        carry_ref[...] = jnp.zeros_like(carry_ref)
        key_pos = lax.broadcasted_iota(jnp.int32, (n_keys, n_cols), 0)
        q_pos = lax.broadcasted_iota(jnp.int32, (n_keys, n_cols), 1) % t_new
        page(kn_ref[...], vn_ref[...], key_pos < q_pos)

    @pl.when(s > 0)
    def _():
        for i in range(p):
            page(k_refs[i][...], v_refs[i][...], None)

    @pl.when(s == pl.num_programs(1) - 1)
    def _():
        for h in range(heads):
            o = acc_ref[h * t_new:(h + 1) * t_new, h * hv:(h + 1) * hv]
            o_ref[:, h * hv:(h + 1) * hv] = (o * _silu(g_ref[:, h * hv:(h + 1) * hv])).astype(o_ref.dtype)


def _sbs_call(us3, cache_k, cache_v, page_table, bias, slot, heads, hk, hv):
    db, t_new, _ = us3.shape
    n_pool, page, n_b = cache_k.shape[:3]
    n_pages = page_table.shape[1]
    qk = heads * hk
    br = heads * hv
    p = 4
    while n_pages % p:
        p //= 2
    q = us3[..., :qk].reshape(db, t_new, heads, hk)
    eye = jnp.eye(heads, dtype=F32)
    wq = (q.transpose(0, 2, 3, 1)[:, :, :, None, :] * eye[None, :, None, :, None])
    wq = wq.reshape(db, qk, heads * t_new).astype(BF16)
    pad = ((0, 0), (0, page - t_new), (0, 0))
    k_new = jnp.pad(us3[..., qk:2 * qk], pad)
    v_new = jnp.pad(us3[..., 2 * qk:2 * qk + br], pad)
    g3 = us3[..., 2 * qk + br:]
    bias_row = jnp.repeat(bias.astype(F32), t_new).reshape(1, heads * t_new)
    ck = cache_k.reshape(n_pool, page, n_b * qk)
    cv = cache_v.reshape(n_pool, page, n_b * br)

    def page_map(i):
        def index_map(b, s, pt):
            logical = n_pages - 1 - (jnp.maximum(s - 1, 0) * p + i)
            return (pt[b, logical], 0, slot)
        return index_map

    kern = functools.partial(_sbs_kernel, pages_per_step=p, t_new=t_new, heads=heads, hv=hv,
                             scale=float(hk) ** -0.5)
    grid_spec = pltpu.PrefetchScalarGridSpec(
        num_scalar_prefetch=1,
        grid=(db, 1 + n_pages // p),
        in_specs=(
            [pl.BlockSpec((1, heads * t_new), lambda b, s, pt: (0, 0)),
             pl.BlockSpec((None, qk, heads * t_new), lambda b, s, pt: (b, 0, 0)),
             pl.BlockSpec((None, page, qk), lambda b, s, pt: (b, 0, 0)),
             pl.BlockSpec((None, page, br), lambda b, s, pt: (b, 0, 0))]
            + [pl.BlockSpec((None, page, qk), page_map(i)) for i in range(p)]
            + [pl.BlockSpec((None, page, br), page_map(i)) for i in range(p)]
            + [pl.BlockSpec((None, t_new, br), lambda b, s, pt: (b, 0, 0))]),
        out_specs=pl.BlockSpec((None, t_new, br), lambda b, s, pt: (b, 0, 0)),
        scratch_shapes=[pltpu.VMEM((heads * t_new, br), F32), pltpu.VMEM((1, heads * t_new), F32)],
    )
    return pl.pallas_call(
        kern,
        out_shape=jax.ShapeDtypeStruct((db, t_new, br), BF16),
        grid_spec=grid_spec,
        compiler_params=pltpu.CompilerParams(
            dimension_semantics=("arbitrary", "arbitrary"), vmem_limit_bytes=_vmem(48)),
        name="sb_decode",
    )(page_table, bias_row, wq, k_new, v_new, *([ck] * p), *([cv] * p), g3)


def kernel(x_prompt, x_sample, state_hgrn, cache_k, cache_v, page_table, c_prompt, c_sample, w_ada, b_ada,
           w_in, w_out, hgrn_lb_logits, hgrn_norm_gain, sb_logit_bias, ln_gain, ln_bias):
    b, t, d = x_prompt.shape
    db, dt, _ = x_sample.shape
    depth = w_in.shape[0]
    _, _, heads, hk, hv = state_hgrn.shape
    qk = heads * hk
    br = heads * hv
    in_width = w_in.shape[-1]
    alpha = (2.0 * depth) ** 0.25

    n_c = b + db
    c_all = jnp.pad(jnp.concatenate([c_prompt, c_sample], axis=0), ((0, (-n_c) % 8), (0, 0)))
    mods = _ada_call(c_all, w_ada, b_ada)
    w_in_bf = w_in.astype(BF16)
    w_out_bf = w_out.astype(BF16)

    tm_p = min(t, 1024)
    tm_o = min(t, 512)
    xp = x_prompt.reshape(b * t, d)
    xs = x_sample.reshape(db * dt, d)
    t_pad = -(-dt // HGRN_CHUNK) * HGRN_CHUNK
    hgrn_p, hgrn_s, kp_rows, vp_rows, ks_rows, vs_rows = [], [], [], [], [], []
    for layer in range(depth):
        slot = layer // N_MIXERS
        m = mods[layer]
        mp = [m[:b, i * d:(i + 1) * d].reshape(b, 1, d) for i in range(3)]
        ms = [jnp.repeat(m[b:n_c, i * d:(i + 1) * d], dt, axis=0).reshape(1, db * dt, d) for i in range(3)]
        up = _inproj_call(xp, mp[0], mp[1], w_in_bf, layer, t, tm_p)
        us = _inproj_call(xs, ms[0], ms[1], w_in_bf, layer, db * dt, db * dt)
        up3 = up.reshape(b, t, in_width)
        us3 = us.reshape(db, dt, in_width)
        if layer % N_MIXERS == 0:
            bp, sp = _hgrn_call(up3, hgrn_lb_logits, hgrn_norm_gain, jnp.zeros((b, heads, hk, hv), F32),
                                slot, heads, hk, hv)
            us_pad = jnp.pad(us3, ((0, 0), (0, t_pad - dt), (0, 0)))
            bs, ss = _hgrn_call(us_pad, hgrn_lb_logits, hgrn_norm_gain, state_hgrn[slot],
                                slot, heads, hk, hv, t_valid=dt)
            bs = bs[:, :dt]
            hgrn_p.append(sp.astype(state_hgrn.dtype))
            hgrn_s.append(ss.astype(state_hgrn.dtype))
        else:
            bias = sb_logit_bias[slot].astype(F32)
            bp = _sbp_call(up3, bias, heads, hk, hv)
            bs = _sbs_call(us3, cache_k, cache_v, page_table, bias, slot, heads, hk, hv)
            kp_rows.append(up3[..., qk:2 * qk].reshape(b, t, heads, hk).astype(cache_k.dtype))
            vp_rows.append(up3[..., 2 * qk:2 * qk + br].reshape(b, t, heads, hv).astype(cache_v.dtype))
            ks_rows.append(us3[..., qk:2 * qk].reshape(db, dt, heads, hk).astype(cache_k.dtype))
            vs_rows.append(us3[..., 2 * qk:2 * qk + br].reshape(db, dt, heads, hv).astype(cache_v.dtype))
        xp = _outproj_call(xp, bp.reshape(b * t, br), w_out_bf, mp[2], ln_gain, ln_bias, layer, t, tm_o, alpha)
        xs = _outproj_call(xs, bs.reshape(db * dt, br), w_out_bf, ms[2], ln_gain, ln_bias, layer,
                           db * dt, db * dt, alpha)
    return (xp.reshape(b, t, d), xs.reshape(db, dt, d),
            jnp.stack(hgrn_p, axis=0), jnp.stack(hgrn_s, axis=0),
            jnp.stack(kp_rows, axis=2), jnp.stack(vp_rows, axis=2),
            jnp.stack(ks_rows, axis=2), jnp.stack(vs_rows, axis=2))
```

```python
import functools

import jax
import jax.numpy as jnp
from jax import lax
from jax.experimental import pallas as pl
from jax.experimental.pallas import tpu as pltpu

F32 = jnp.float32
BF16 = jnp.bfloat16

LN_EPS = 1e-5
RMS_EPS = 1e-6
HGRN_CHUNK = 64
HGRN_TBLK = 256
HGRN_HEADS_PER_STEP = 4
SBP_TQ = 1024
SBP_TK = 256
N_MIXERS = 2

NT_DIMS = (((1,), (1,)), ((), ()))
TN_DIMS = (((0,), (0,)), ((), ()))


def _vmem(mib):
    return mib * 1024 * 1024


def _sigmoid(x):
    return 1.0 / (1.0 + jnp.exp(-x))


def _silu(x):
    return x * _sigmoid(x)


def _softplus(z):
    return jnp.maximum(z, 0.0) + jnp.log(1.0 + jnp.exp(-jnp.abs(z)))


def _split2(x):
    hi = x.astype(BF16)
    lo = (x - hi.astype(F32)).astype(BF16)
    return hi, lo


def _split3(x):
    hi = x.astype(BF16)
    r = x - hi.astype(F32)
    mid = r.astype(BF16)
    lo = (r - mid.astype(F32)).astype(BF16)
    return hi, mid, lo


def _ada_kernel(c_ref, w_ref, b_ref, o_ref):
    o_ref[...] = jnp.dot(c_ref[...], w_ref[...], precision=lax.Precision.HIGHEST,
                         preferred_element_type=F32) + b_ref[...]


def _ada_call(c_all, w_ada, b_ada):
    depth, d, n = w_ada.shape
    rows = c_all.shape[0]
    tn = 1024
    return pl.pallas_call(
        _ada_kernel,
        out_shape=jax.ShapeDtypeStruct((depth, rows, n), F32),
        grid=(depth, n // tn),
        in_specs=[
            pl.BlockSpec((rows, d), lambda l, j: (0, 0)),
            pl.BlockSpec((None, d, tn), lambda l, j: (l, 0, j)),
            pl.BlockSpec((None, 1, tn), lambda l, j: (l, 0, j)),
        ],
        out_specs=pl.BlockSpec((None, rows, tn), lambda l, j: (l, 0, j)),
        compiler_params=pltpu.CompilerParams(
            dimension_semantics=("arbitrary", "arbitrary"), vmem_limit_bytes=_vmem(40)),
        name="ada_mod",
    )(c_all, w_ada, b_ada.reshape(depth, 1, n))


def _inproj_kernel(x_ref, shift_ref, scale_ref, w_ref, u_ref, h_ref):
    @pl.when(pl.program_id(1) == 0)
    def _():
        h_ref[...] = (x_ref[...] * (1.0 + scale_ref[...]) + shift_ref[...]).astype(BF16)

    u_ref[...] = jnp.dot(h_ref[...], w_ref[...], preferred_element_type=F32)


def _inproj_call(x2, shift, scale, w_bf, layer, rows_per_mod, tm):
    m, d = x2.shape
    n = w_bf.shape[-1]
    tn = 1024
    r = shift.shape[1]
    tiles_per_mod = rows_per_mod // tm
    mod_spec = pl.BlockSpec((None, r, d), lambda i, j: (i // tiles_per_mod, 0, 0))
    return pl.pallas_call(
        _inproj_kernel,
        out_shape=jax.ShapeDtypeStruct((m, n), F32),
        grid=(m // tm, n // tn),
        in_specs=[
            pl.BlockSpec((tm, d), lambda i, j: (i, 0)),
            mod_spec, mod_spec,
            pl.BlockSpec((None, d, tn), lambda i, j: (layer, 0, j)),
        ],
        out_specs=pl.BlockSpec((tm, tn), lambda i, j: (i, j)),
        scratch_shapes=[pltpu.VMEM((tm, d), BF16)],
        compiler_params=pltpu.CompilerParams(
            dimension_semantics=("arbitrary", "arbitrary"), vmem_limit_bytes=_vmem(48)),
        name="in_proj",
    )(x2, shift, scale, w_bf)


def _outproj_kernel(x_ref, br_ref, w_ref, gate_ref, lng_ref, lnb_ref, y_ref, acc_ref, *, alpha, nk):
    k = pl.program_id(1)

    @pl.when(k == 0)
    def _():
        acc_ref[...] = jnp.zeros_like(acc_ref)

    acc_ref[...] += jnp.dot(br_ref[...], w_ref[...], preferred_element_type=F32)

    @pl.when(k == nk - 1)
    def _():
        v = alpha * x_ref[...] + gate_ref[...] * acc_ref[...]
        mu = jnp.mean(v, axis=-1, keepdims=True)
        dv = v - mu
        var = jnp.mean(dv * dv, axis=-1, keepdims=True)
        y_ref[...] = dv * lax.rsqrt(var + LN_EPS) * lng_ref[...] + lnb_ref[...]


def _outproj_call(x2, br2, w_bf, gate, ln_gain, ln_bias, layer, rows_per_mod, tm, alpha):
    m, d = x2.shape
    kdim = br2.shape[1]
    tk = 2048
    nk = kdim // tk
    r = gate.shape[1]
    tiles_per_mod = rows_per_mod // tm
    depth = ln_gain.shape[0]
    vec_spec = pl.BlockSpec((None, 1, d), lambda i, k: (layer, 0, 0))
    return pl.pallas_call(
        functools.partial(_outproj_kernel, alpha=alpha, nk=nk),
        out_shape=jax.ShapeDtypeStruct((m, d), F32),
        grid=(m // tm, nk),
        in_specs=[
            pl.BlockSpec((tm, d), lambda i, k: (i, 0)),
            pl.BlockSpec((tm, tk), lambda i, k: (i, k)),
            pl.BlockSpec((None, tk, d), lambda i, k: (layer, k, 0)),
            pl.BlockSpec((None, r, d), lambda i, k: (i // tiles_per_mod, 0, 0)),
            vec_spec, vec_spec,
        ],
        out_specs=pl.BlockSpec((tm, d), lambda i, k: (i, 0)),
        scratch_shapes=[pltpu.VMEM((tm, d), F32)],
        compiler_params=pltpu.CompilerParams(
            dimension_semantics=("arbitrary", "arbitrary"), vmem_limit_bytes=_vmem(48)),
        name="out_proj_ln",
    )(x2, br2, w_bf, gate, ln_gain.reshape(depth, 1, d), ln_bias.reshape(depth, 1, d))


def _hgrn_kernel(lbl_ref, gain_ref, q_ref, f_ref, i_ref, g_ref, s0_ref, o_ref, s_ref, st_ref, *,
                 slot, chunk, n_chunks, n_heads, hk, hv, t_valid):
    tb = pl.program_id(2)
    c = chunk

    @pl.when(tb == 0)
    def _():
        for hh in range(n_heads):
            st_ref[hh] = s0_ref[hh].T

    lg = lbl_ref[...]
    e = jnp.exp(lg - jnp.max(lg, axis=0, keepdims=True))
    lb_all = jnp.sum(e[:slot + 1], axis=0, keepdims=True) / jnp.sum(e, axis=0, keepdims=True)
    gain = gain_ref[...]

    row = lax.broadcasted_iota(jnp.int32, (c, c), 0)
    col = lax.broadcasted_iota(jnp.int32, (c, c), 1)
    causal = col <= row
    tri = jnp.where(causal, 1.0, 0.0).astype(BF16)
    tri3 = jnp.concatenate([tri, tri, tri], axis=1)

    for ci in range(n_chunks):
        sl = pl.ds(ci * c, c)
        for hh in range(n_heads):
            ks = pl.ds(hh * hk, hk)
            vs = pl.ds(hh * hv, hv)
            lb = lb_all[:, hh * hk:(hh + 1) * hk]
            zf = f_ref[sl, ks]
            t = jnp.exp(-jnp.abs(zf))
            r = 1.0 / (1.0 + t)
            pos = zf >= 0.0
            sig = jnp.where(pos, r, t * r)
            nsig = jnp.where(pos, t * r, r)
            log_f = jnp.log(lb + (1.0 - lb) * sig)
            k = (1.0 - lb) * nsig
            if t_valid is not None:
                tpos = (tb * n_chunks + ci) * c + lax.broadcasted_iota(jnp.int32, (c, 1), 0)
                valid = tpos < t_valid
                log_f = jnp.where(valid, log_f, 0.0)
                k = jnp.where(valid, k, 0.0)
            cum = jnp.dot(tri3, jnp.concatenate(_split3(log_f), axis=0),
                          preferred_element_type=F32)
            ref = cum[c // 2:c // 2 + 1, :]
            last = cum[c - 1:c, :]
            q = _silu(q_ref[sl, ks])
            qe = (q * jnp.exp(cum - ref)).astype(BF16)
            ke = (k * jnp.exp(ref - cum)).astype(BF16)
            qd = (q * jnp.exp(cum)).astype(BF16)
            kd = (k * jnp.exp(last - cum)).astype(BF16)
            vb = i_ref[sl, vs].astype(BF16)
            scores = lax.dot_general(qe, ke, NT_DIMS, preferred_element_type=F32)
            scores = jnp.where(causal, scores, 0.0).astype(BF16)
            st = st_ref[hh]
            o = (jnp.dot(scores, vb, preferred_element_type=F32)
                 + lax.dot_general(qd, st.astype(BF16), NT_DIMS, preferred_element_type=F32))
            st_ref[hh] = jnp.exp(last) * st + lax.dot_general(vb, kd, TN_DIMS, preferred_element_type=F32)
            ms = jnp.mean(o * o, axis=-1, keepdims=True)
            on = o * lax.rsqrt(ms + RMS_EPS) * gain
            o_ref[sl, vs] = (on * _silu(g_ref[sl, vs])).astype(o_ref.dtype)

    @pl.when(tb == pl.num_programs(2) - 1)
    def _():
        for hh in range(n_heads):
            s_ref[hh] = st_ref[hh].T


def _hgrn_call(u3, lb_logits, norm_gain, s0, slot, heads, hk, hv, t_valid=None):
    b, t, _ = u3.shape
    tblk = min(t, HGRN_TBLK)
    n_chunks = tblk // HGRN_CHUNK
    hb = HGRN_HEADS_PER_STEP
    groups = heads // hb
    qk = heads * hk
    v_off = 2 * qk // (hb * hv)
    n_lb = lb_logits.shape[0]
    kern = functools.partial(_hgrn_kernel, slot=slot, chunk=HGRN_CHUNK, n_chunks=n_chunks,
                             n_heads=hb, hk=hk, hv=hv, t_valid=t_valid)
    return pl.pallas_call(
        kern,
        out_shape=(jax.ShapeDtypeStruct((b, t, heads * hv), BF16),
                   jax.ShapeDtypeStruct((b, heads, hk, hv), F32)),
        grid=(b, groups, t // tblk),
        in_specs=[
            pl.BlockSpec((n_lb, hb * hk), lambda bi, g, tb: (0, g)),
            pl.BlockSpec((None, 1, hv), lambda bi, g, tb: (slot, 0, 0)),
            pl.BlockSpec((None, tblk, hb * hk), lambda bi, g, tb: (bi, tb, g)),
            pl.BlockSpec((None, tblk, hb * hk), lambda bi, g, tb: (bi, tb, groups + g)),
            pl.BlockSpec((None, tblk, hb * hv), lambda bi, g, tb: (bi, tb, v_off + g)),
            pl.BlockSpec((None, tblk, hb * hv), lambda bi, g, tb: (bi, tb, v_off + groups + g)),
            pl.BlockSpec((None, hb, hk, hv), lambda bi, g, tb: (bi, g, 0, 0)),
        ],
        out_specs=(
            pl.BlockSpec((None, tblk, hb * hv), lambda bi, g, tb: (bi, tb, g)),
            pl.BlockSpec((None, hb, hk, hv), lambda bi, g, tb: (bi, g, 0, 0)),
        ),
        scratch_shapes=[pltpu.VMEM((hb, hv, hk), F32)],
        compiler_params=pltpu.CompilerParams(
            dimension_semantics=("arbitrary", "arbitrary", "arbitrary"), vmem_limit_bytes=_vmem(40)),
        name="hgrn_mixer",
    )(lb_logits, norm_gain.reshape(-1, 1, hv), u3, u3, u3, u3, s0)


def _sbp_kernel(bias_ref, q_ref, k_ref, v_ref, g_ref, o_ref, acc_ref, *, tq, tk, scale):
    h = pl.program_id(1)
    qi = pl.program_id(2)
    bias = bias_ref[h]
    q = (q_ref[...] * scale).astype(BF16)
    row = lax.broadcasted_iota(jnp.int32, (tk, tk), 0)
    col = lax.broadcasted_iota(jnp.int32, (tk, tk), 1)
    suffix = jnp.where(row >= col, 1.0, 0.0).astype(BF16)
    suffix2 = jnp.concatenate([suffix, suffix], axis=0)
    acc_ref[...] = jnp.zeros_like(acc_ref)

    def pair(start, carry, r0):
        masked = r0 is not None
        r0 = r0 or 0
        rows = tq - r0
        kb = k_ref[pl.ds(start, 2 * tk), :].astype(BF16)
        vb = v_ref[pl.ds(start, 2 * tk), :].astype(BF16)
        z = lax.dot_general(q[r0:], kb, NT_DIMS, preferred_element_type=F32) + bias
        sp = _softplus(z)
        if masked:
            q_pos = qi * tq + r0 + lax.broadcasted_iota(jnp.int32, (rows, 2 * tk), 0)
            k_pos = start + lax.broadcasted_iota(jnp.int32, (rows, 2 * tk), 1)
            keep = k_pos < q_pos
            sp = jnp.where(keep, sp, 0.0)
        hi, lo = _split2(sp)
        rs_new = jnp.dot(jnp.concatenate([hi[:, tk:], lo[:, tk:]], axis=1), suffix2,
                         preferred_element_type=F32)
        rs_old = jnp.dot(jnp.concatenate([hi[:, :tk], lo[:, :tk]], axis=1), suffix2,
                         preferred_element_type=F32)
        carry_old = carry + rs_new[:, 0:1]
        w = jnp.concatenate([jnp.exp(z[:, :tk] - rs_old - carry_old),
                             jnp.exp(z[:, tk:] - rs_new - carry)], axis=1)
        if masked:
            w = jnp.where(keep, w, 0.0)
        acc_ref[r0:, :] += jnp.dot(w.astype(BF16), vb, preferred_element_type=F32)
        return carry_old + rs_old[:, 0:1]

    span = 2 * tk
    n_diag = tq // span
    carry = jnp.zeros((tq, 1), F32)
    for dj in range(n_diag):
        r0 = (n_diag - 1 - dj) * span
        c_new = pair(pl.multiple_of(qi * tq + r0, span), carry[r0:], r0)
        carry = c_new if r0 == 0 else jnp.concatenate([carry[:r0], c_new], axis=0)

    def body(jj, c):
        return pair(pl.multiple_of(qi * tq - (jj + 1) * span, span), c, None)

    lax.fori_loop(0, qi * n_diag, body, carry)
    o_ref[...] = (acc_ref[...] * _silu(g_ref[...])).astype(o_ref.dtype)


def _sbp_call(u3, bias, heads, hk, hv):
    b, t, _ = u3.shape
    tq = min(t, SBP_TQ)
    tk = min(tq // 2, SBP_TK)
    qk = heads * hk
    kv = qk // hv
    kern = functools.partial(_sbp_kernel, tq=tq, tk=tk, scale=float(hk) ** -0.5)
    grid_spec = pltpu.PrefetchScalarGridSpec(
        num_scalar_prefetch=1,
        grid=(b, heads, t // tq),
        in_specs=[
            pl.BlockSpec((None, tq, hk), lambda bi, h, qi, bias: (bi, qi, h)),
            pl.BlockSpec((None, t, hk), lambda bi, h, qi, bias: (bi, 0, heads + h)),
            pl.BlockSpec((None, t, hv), lambda bi, h, qi, bias: (bi, 0, 2 * kv + h)),
            pl.BlockSpec((None, tq, hv), lambda bi, h, qi, bias: (bi, qi, 2 * kv + heads + h)),
        ],
        out_specs=pl.BlockSpec((None, tq, hv), lambda bi, h, qi, bias: (bi, qi, h)),
        scratch_shapes=[pltpu.VMEM((tq, hv), F32)],
    )
    return pl.pallas_call(
        kern,
        out_shape=jax.ShapeDtypeStruct((b, t, heads * hv), BF16),
        grid_spec=grid_spec,
        compiler_params=pltpu.CompilerParams(
            dimension_semantics=("arbitrary", "arbitrary", "arbitrary"), vmem_limit_bytes=_vmem(48)),
        name="sb_prompt",
    )(bias, u3, u3, u3, u3)


def _sbs_kernel(pt_ref, bias_ref, wq_ref, kn_ref, vn_ref, *rest, pages_per_step, t_new, heads, hv,
                row_stride, row_start, scale):
    p = pages_per_step
    k_refs = rest[:p]
    v_refs = rest[p:3 * p]
    g_ref, o_ref, acc_ref, carry_ref = rest[3 * p:]
    s = pl.program_id(1)
    n_keys = kn_ref.shape[0]
    n_cols = heads * t_new

    row = lax.broadcasted_iota(jnp.int32, (n_keys, n_keys), 0)
    col = lax.broadcasted_iota(jnp.int32, (n_keys, n_keys), 1)
    newer = jnp.where(col > row, 1.0, 0.0).astype(BF16)
    newer2 = jnp.concatenate([newer, newer], axis=1)

    def page(kflat, vflat, mask):
        zt = jnp.dot(kflat, wq_ref[...], preferred_element_type=F32) * scale + bias_ref[...]
        sp = _softplus(zt)
        if mask is not None:
            sp = jnp.where(mask, sp, 0.0)
        hi, lo = _split2(sp)
        later = jnp.dot(newer2, jnp.concatenate([hi, lo], axis=0), preferred_element_type=F32)
        w = jnp.exp(zt - sp - later - carry_ref[...])
        if mask is not None:
            w = jnp.where(mask, w, 0.0)
        acc_ref[...] += lax.dot_general(w.astype(BF16), vflat, TN_DIMS,
                                        preferred_element_type=F32)
        carry_ref[...] += later[0:1, :] + sp[0:1, :]

    def head_rows(ref, h):
        return ref[pl.ds(row_start + h, n_keys, stride=row_stride), :]

    @pl.when(s == 0)
    def _():
        acc_ref[...] = jnp.zeros_like(acc_ref)
        carry_ref[...] = jnp.zeros_like(carry_ref)
        key_pos = lax.broadcasted_iota(jnp.int32, (n_keys, n_cols), 0)
        q_pos = lax.broadcasted_iota(jnp.int32, (n_keys, n_cols), 1) % t_new
        page(kn_ref[...].astype(BF16), vn_ref[...].astype(BF16), key_pos < q_pos)

    @pl.when(s > 0)
    def _():
        for i in range(p):
            kflat = jnp.concatenate([head_rows(k_refs[i], h).astype(BF16) for h in range(heads)], axis=1)
            vflat = jnp.concatenate(
                [head_rows(v_refs[2 * i + half], h).astype(BF16) for h in range(heads) for half in range(2)],
                axis=1)
            page(kflat, vflat, None)

    @pl.when(s == pl.num_programs(1) - 1)
    def _():
        for h in range(heads):
            o = acc_ref[h * t_new:(h + 1) * t_new, h * hv:(h + 1) * hv]
            o_ref[:, h * hv:(h + 1) * hv] = (o * _silu(g_ref[:, h * hv:(h + 1) * hv])).astype(o_ref.dtype)


def _sbs_call(us3, cache_k, cache_v, page_table, bias, slot, heads, hk, hv):
    db, t_new, _ = us3.shape
    n_pool, page, n_b = cache_k.shape[:3]
    n_pages = page_table.shape[1]
    qk = heads * hk
    br = heads * hv
    half = hv // 2
    p = 4
    while n_pages % p:
        p //= 2
    q = us3[..., :qk].reshape(db, t_new, heads, hk)
    eye = jnp.eye(heads, dtype=F32)
    wq = (q.transpose(0, 2, 3, 1)[:, :, :, None, :] * eye[None, :, None, :, None])
    wq = wq.reshape(db, qk, heads * t_new).astype(BF16)
    pad = ((0, 0), (0, page - t_new), (0, 0))
    k_new = jnp.pad(us3[..., qk:2 * qk], pad)
    v_new = jnp.pad(us3[..., 2 * qk:2 * qk + br], pad)
    g3 = us3[..., 2 * qk + br:]
    bias_row = jnp.repeat(bias.astype(F32), t_new).reshape(1, heads * t_new)
    rows = page * n_b * heads
    ck = cache_k.reshape(n_pool, rows, hk)
    cv = cache_v.reshape(n_pool, rows, hv)

    def page_map(i, lane_block):
        def index_map(b, s, pt):
            logical = n_pages - 1 - (jnp.maximum(s - 1, 0) * p + i)
            return (pt[b, logical], 0, lane_block)
        return index_map

    kern = functools.partial(_sbs_kernel, pages_per_step=p, t_new=t_new, heads=heads, hv=hv,
                             row_stride=n_b * heads, row_start=slot * heads, scale=float(hk) ** -0.5)
    grid_spec = pltpu.PrefetchScalarGridSpec(
        num_scalar_prefetch=1,
        grid=(db, 1 + n_pages // p),
        in_specs=(
            [pl.BlockSpec((1, heads * t_new), lambda b, s, pt: (0, 0)),
             pl.BlockSpec((None, qk, heads * t_new), lambda b, s, pt: (b, 0, 0)),
             pl.BlockSpec((None, page, qk), lambda b, s, pt: (b, 0, 0)),
             pl.BlockSpec((None, page, br), lambda b, s, pt: (b, 0, 0))]
            + [pl.BlockSpec((None, rows, hk), page_map(i, 0)) for i in range(p)]
            + [pl.BlockSpec((None, rows, half), page_map(i, lb)) for i in range(p) for lb in range(2)]
            + [pl.BlockSpec((None, t_new, br), lambda b, s, pt: (b, 0, 0))]),
        out_specs=pl.BlockSpec((None, t_new, br), lambda b, s, pt: (b, 0, 0)),
        scratch_shapes=[pltpu.VMEM((heads * t_new, br), F32), pltpu.VMEM((1, heads * t_new), F32)],
    )
    return pl.pallas_call(
        kern,
        out_shape=jax.ShapeDtypeStruct((db, t_new, br), BF16),
        grid_spec=grid_spec,
        compiler_params=pltpu.CompilerParams(
            dimension_semantics=("arbitrary", "arbitrary"), vmem_limit_bytes=_vmem(48)),
        name="sb_decode",
    )(page_table, bias_row, wq, k_new, v_new, *([ck] * p), *([cv] * (2 * p)), g3)


def kernel(x_prompt, x_sample, state_hgrn, cache_k, cache_v, page_table, c_prompt, c_sample, w_ada, b_ada,
           w_in, w_out, hgrn_lb_logits, hgrn_norm_gain, sb_logit_bias, ln_gain, ln_bias):
    b, t, d = x_prompt.shape
    db, dt, _ = x_sample.shape
    depth = w_in.shape[0]
    _, _, heads, hk, hv = state_hgrn.shape
    qk = heads * hk
    br = heads * hv
    in_width = w_in.shape[-1]
    alpha = (2.0 * depth) ** 0.25

    n_c = b + db
    c_all = jnp.pad(jnp.concatenate([c_prompt, c_sample], axis=0), ((0, (-n_c) % 8), (0, 0)))
    mods = _ada_call(c_all, w_ada, b_ada)
    w_in_bf = w_in.astype(BF16)
    w_out_bf = w_out.astype(BF16)

    tm_p = min(t, 1024)
    tm_o = min(t, 512)
    xp = x_prompt.reshape(b * t, d)
    xs = x_sample.reshape(db * dt, d)
    t_pad = -(-dt // HGRN_CHUNK) * HGRN_CHUNK
    hgrn_p, hgrn_s, kp_rows, vp_rows, ks_rows, vs_rows = [], [], [], [], [], []
    for layer in range(depth):
        slot = layer // N_MIXERS
        m = mods[layer]
        mp = [m[:b, i * d:(i + 1) * d].reshape(b, 1, d) for i in range(3)]
        ms = [jnp.repeat(m[b:n_c, i * d:(i + 1) * d], dt, axis=0).reshape(1, db * dt, d) for i in range(3)]
        up = _inproj_call(xp, mp[0], mp[1], w_in_bf, layer, t, tm_p)
        us = _inproj_call(xs, ms[0], ms[1], w_in_bf, layer, db * dt, db * dt)
        up3 = up.reshape(b, t, in_width)
        us3 = us.reshape(db, dt, in_width)
        if layer % N_MIXERS == 0:
            bp, sp = _hgrn_call(up3, hgrn_lb_logits, hgrn_norm_gain, jnp.zeros((b, heads, hk, hv), F32),
                                slot, heads, hk, hv)
            us_pad = jnp.pad(us3, ((0, 0), (0, t_pad - dt), (0, 0)))
            bs, ss = _hgrn_call(us_pad, hgrn_lb_logits, hgrn_norm_gain, state_hgrn[slot],
                                slot, heads, hk, hv, t_valid=dt)
            bs = bs[:, :dt]
            hgrn_p.append(sp.astype(state_hgrn.dtype))
            hgrn_s.append(ss.astype(state_hgrn.dtype))
        else:
            bias = sb_logit_bias[slot].astype(F32)
            bp = _sbp_call(up3, bias, heads, hk, hv)
            bs = _sbs_call(us3, cache_k, cache_v, page_table, bias, slot, heads, hk, hv)
            kp_rows.append(up3[..., qk:2 * qk].reshape(b, t, heads, hk).astype(cache_k.dtype))
            vp_rows.append(up3[..., 2 * qk:2 * qk + br].reshape(b, t, heads, hv).astype(cache_v.dtype))
            ks_rows.append(us3[..., qk:2 * qk].reshape(db, dt, heads, hk).astype(cache_k.dtype))
            vs_rows.append(us3[..., 2 * qk:2 * qk + br].reshape(db, dt, heads, hv).astype(cache_v.dtype))
        xp = _outproj_call(xp, bp.reshape(b * t, br), w_out_bf, mp[2], ln_gain, ln_bias, layer, t, tm_o, alpha)
        xs = _outproj_call(xs, bs.reshape(db * dt, br), w_out_bf, ms[2], ln_gain, ln_bias, layer,
                           db * dt, db * dt, alpha)
    return (xp.reshape(b, t, d), xs.reshape(db, dt, d),
            jnp.stack(hgrn_p, axis=0), jnp.stack(hgrn_s, axis=0),
            jnp.stack(kp_rows, axis=2), jnp.stack(vp_rows, axis=2),
            jnp.stack(ks_rows, axis=2), jnp.stack(vs_rows, axis=2))
```

```python
import functools

import jax
import jax.numpy as jnp
from jax import lax
from jax.experimental import pallas as pl
from jax.experimental.pallas import tpu as pltpu

F32 = jnp.float32
BF16 = jnp.bfloat16

LN_EPS = 1e-5
RMS_EPS = 1e-6
HGRN_CHUNK = 64
HGRN_TBLK = 128
HGRN_HEADS_PER_STEP = 16
SBP_TQ = 1024
SBP_TK = 256
N_MIXERS = 2

NT_DIMS = (((1,), (1,)), ((), ()))
TN_DIMS = (((0,), (0,)), ((), ()))


def _vmem(mib):
    return mib * 1024 * 1024


def _sigmoid(x):
    return 1.0 / (1.0 + jnp.exp(-x))


def _silu(x):
    return x * _sigmoid(x)


def _softplus(z):
    return jnp.maximum(z, 0.0) + jnp.log(1.0 + jnp.exp(-jnp.abs(z)))


def _split2(x):
    hi = x.astype(BF16)
    lo = (x - hi.astype(F32)).astype(BF16)
    return hi, lo


def _split3(x):
    hi = x.astype(BF16)
    r = x - hi.astype(F32)
    mid = r.astype(BF16)
    lo = (r - mid.astype(F32)).astype(BF16)
    return hi, mid, lo


def _ada_kernel(c_ref, w_ref, b_ref, o_ref):
    o_ref[...] = jnp.dot(c_ref[...], w_ref[...], precision=lax.Precision.HIGHEST,
                         preferred_element_type=F32) + b_ref[...]


def _ada_call(c_all, w_ada, b_ada):
    depth, d, n = w_ada.shape
    rows = c_all.shape[0]
    tn = 1024
    return pl.pallas_call(
        _ada_kernel,
        out_shape=jax.ShapeDtypeStruct((depth, rows, n), F32),
        grid=(depth, n // tn),
        in_specs=[
            pl.BlockSpec((rows, d), lambda l, j: (0, 0)),
            pl.BlockSpec((None, d, tn), lambda l, j: (l, 0, j)),
            pl.BlockSpec((None, 1, tn), lambda l, j: (l, 0, j)),
        ],
        out_specs=pl.BlockSpec((None, rows, tn), lambda l, j: (l, 0, j)),
        compiler_params=pltpu.CompilerParams(
            dimension_semantics=("arbitrary", "arbitrary"), vmem_limit_bytes=_vmem(40)),
        name="ada_mod",
    )(c_all, w_ada, b_ada.reshape(depth, 1, n))


def _inproj_kernel(x_ref, shift_ref, scale_ref, w_ref, u_ref, h_ref):
    @pl.when(pl.program_id(1) == 0)
    def _():
        h_ref[...] = (x_ref[...] * (1.0 + scale_ref[...]) + shift_ref[...]).astype(BF16)

    u_ref[...] = jnp.dot(h_ref[...], w_ref[...], preferred_element_type=F32)


def _inproj_call(x2, shift, scale, w_bf, layer, rows_per_mod, tm):
    m, d = x2.shape
    n = w_bf.shape[-1]
    tn = 1024
    r = shift.shape[1]
    tiles_per_mod = rows_per_mod // tm
    mod_spec = pl.BlockSpec((None, r, d), lambda i, j: (i // tiles_per_mod, 0, 0))
    return pl.pallas_call(
        _inproj_kernel,
        out_shape=jax.ShapeDtypeStruct((m, n), F32),
        grid=(m // tm, n // tn),
        in_specs=[
            pl.BlockSpec((tm, d), lambda i, j: (i, 0)),
            mod_spec, mod_spec,
            pl.BlockSpec((None, d, tn), lambda i, j: (layer, 0, j)),
        ],
        out_specs=pl.BlockSpec((tm, tn), lambda i, j: (i, j)),
        scratch_shapes=[pltpu.VMEM((tm, d), BF16)],
        compiler_params=pltpu.CompilerParams(
            dimension_semantics=("arbitrary", "arbitrary"), vmem_limit_bytes=_vmem(48)),
        name="in_proj",
    )(x2, shift, scale, w_bf)


def _outproj_kernel(x_ref, br_ref, w_ref, gate_ref, lng_ref, lnb_ref, y_ref, acc_ref, *, alpha, nk):
    k = pl.program_id(1)

    @pl.when(k == 0)
    def _():
        acc_ref[...] = jnp.zeros_like(acc_ref)

    acc_ref[...] += jnp.dot(br_ref[...], w_ref[...], preferred_element_type=F32)

    @pl.when(k == nk - 1)
    def _():
        v = alpha * x_ref[...] + gate_ref[...] * acc_ref[...]
        mu = jnp.mean(v, axis=-1, keepdims=True)
        dv = v - mu
        var = jnp.mean(dv * dv, axis=-1, keepdims=True)
        y_ref[...] = dv * lax.rsqrt(var + LN_EPS) * lng_ref[...] + lnb_ref[...]


def _outproj_call(x2, br2, w_bf, gate, ln_gain, ln_bias, layer, rows_per_mod, tm, alpha):
    m, d = x2.shape
    kdim = br2.shape[1]
    tk = 2048
    nk = kdim // tk
    r = gate.shape[1]
    tiles_per_mod = rows_per_mod // tm
    depth = ln_gain.shape[0]
    vec_spec = pl.BlockSpec((None, 1, d), lambda i, k: (layer, 0, 0))
    return pl.pallas_call(
        functools.partial(_outproj_kernel, alpha=alpha, nk=nk),
        out_shape=jax.ShapeDtypeStruct((m, d), F32),
        grid=(m // tm, nk),
        in_specs=[
            pl.BlockSpec((tm, d), lambda i, k: (i, 0)),
            pl.BlockSpec((tm, tk), lambda i, k: (i, k)),
            pl.BlockSpec((None, tk, d), lambda i, k: (layer, k, 0)),
            pl.BlockSpec((None, r, d), lambda i, k: (i // tiles_per_mod, 0, 0)),
            vec_spec, vec_spec,
        ],
        out_specs=pl.BlockSpec((tm, d), lambda i, k: (i, 0)),
        scratch_shapes=[pltpu.VMEM((tm, d), F32)],
        compiler_params=pltpu.CompilerParams(
            dimension_semantics=("arbitrary", "arbitrary"), vmem_limit_bytes=_vmem(48)),
        name="out_proj_ln",
    )(x2, br2, w_bf, gate, ln_gain.reshape(depth, 1, d), ln_bias.reshape(depth, 1, d))


def _hgrn_kernel(lbl_ref, gain_ref, q_ref, f_ref, i_ref, g_ref, s0_ref, o_ref, s_ref, st_ref, *,
                 slot, chunk, n_chunks, n_heads, hk, hv, t_valid):
    tb = pl.program_id(2)
    c = chunk

    @pl.when(tb == 0)
    def _():
        for hh in range(n_heads):
            st_ref[hh] = s0_ref[hh].T

    lg = lbl_ref[...]
    e = jnp.exp(lg - jnp.max(lg, axis=0, keepdims=True))
    lb_all = jnp.sum(e[:slot + 1], axis=0, keepdims=True) / jnp.sum(e, axis=0, keepdims=True)
    gain = gain_ref[...]

    row = lax.broadcasted_iota(jnp.int32, (c, c), 0)
    col = lax.broadcasted_iota(jnp.int32, (c, c), 1)
    causal = col <= row
    tri = jnp.where(causal, 1.0, 0.0).astype(BF16)
    tri3 = jnp.concatenate([tri, tri, tri], axis=1)

    gain_all = jnp.concatenate([gain] * n_heads, axis=1)
    for ci in range(n_chunks):
        sl = pl.ds(ci * c, c)
        zf = f_ref[sl, :]
        t = jnp.exp(-jnp.abs(zf))
        r = 1.0 / (1.0 + t)
        pos = zf >= 0.0
        sig = jnp.where(pos, r, t * r)
        nsig = jnp.where(pos, t * r, r)
        log_f = jnp.log(lb_all + (1.0 - lb_all) * sig)
        k = (1.0 - lb_all) * nsig
        if t_valid is not None:
            tpos = (tb * n_chunks + ci) * c + lax.broadcasted_iota(jnp.int32, (c, 1), 0)
            valid = tpos < t_valid
            log_f = jnp.where(valid, log_f, 0.0)
            k = jnp.where(valid, k, 0.0)
        cum = jnp.dot(tri3, jnp.concatenate(_split3(log_f), axis=0),
                      preferred_element_type=F32)
        ref = cum[c // 2:c // 2 + 1, :]
        last = cum[c - 1:c, :]
        q = _silu(q_ref[sl, :])
        qe = (q * jnp.exp(cum - ref)).astype(BF16)
        ke = (k * jnp.exp(ref - cum)).astype(BF16)
        qd = (q * jnp.exp(cum)).astype(BF16)
        kd = (k * jnp.exp(last - cum)).astype(BF16)
        decay = jnp.exp(last)
        vb_all = i_ref[sl, :].astype(BF16)
        normed = []
        for hh in range(n_heads):
            ks = slice(hh * hk, (hh + 1) * hk)
            vb = vb_all[:, hh * hv:(hh + 1) * hv]
            scores = lax.dot_general(qe[:, ks], ke[:, ks], NT_DIMS, preferred_element_type=F32)
            scores = jnp.where(causal, scores, 0.0).astype(BF16)
            st = st_ref[hh]
            o = (jnp.dot(scores, vb, preferred_element_type=F32)
                 + lax.dot_general(qd[:, ks], st.astype(BF16), NT_DIMS, preferred_element_type=F32))
            st_ref[hh] = decay[:, ks] * st + lax.dot_general(vb, kd[:, ks], TN_DIMS,
                                                            preferred_element_type=F32)
            ms = jnp.mean(o * o, axis=-1, keepdims=True)
            normed.append(o * lax.rsqrt(ms + RMS_EPS))
        on = jnp.concatenate(normed, axis=1)
        o_ref[sl, :] = (on * gain_all * _silu(g_ref[sl, :])).astype(o_ref.dtype)

    @pl.when(tb == pl.num_programs(2) - 1)
    def _():
        for hh in range(n_heads):
            s_ref[hh] = st_ref[hh].T


def _hgrn_call(u3, lb_logits, norm_gain, s0, slot, heads, hk, hv, t_valid=None):
    b, t, _ = u3.shape
    tblk = min(t, HGRN_TBLK)
    n_chunks = tblk // HGRN_CHUNK
    hb = HGRN_HEADS_PER_STEP
    groups = heads // hb
    qk = heads * hk
    v_off = 2 * qk // (hb * hv)
    n_lb = lb_logits.shape[0]
    kern = functools.partial(_hgrn_kernel, slot=slot, chunk=HGRN_CHUNK, n_chunks=n_chunks,
                             n_heads=hb, hk=hk, hv=hv, t_valid=t_valid)
    return pl.pallas_call(
        kern,
        out_shape=(jax.ShapeDtypeStruct((b, t, heads * hv), BF16),
                   jax.ShapeDtypeStruct((b, heads, hk, hv), F32)),
        grid=(b, groups, t // tblk),
        in_specs=[
            pl.BlockSpec((n_lb, hb * hk), lambda bi, g, tb: (0, g)),
            pl.BlockSpec((None, 1, hv), lambda bi, g, tb: (slot, 0, 0)),
            pl.BlockSpec((None, tblk, hb * hk), lambda bi, g, tb: (bi, tb, g)),
            pl.BlockSpec((None, tblk, hb * hk), lambda bi, g, tb: (bi, tb, groups + g)),
            pl.BlockSpec((None, tblk, hb * hv), lambda bi, g, tb: (bi, tb, v_off + g)),
            pl.BlockSpec((None, tblk, hb * hv), lambda bi, g, tb: (bi, tb, v_off + groups + g)),
            pl.BlockSpec((None, hb, hk, hv), lambda bi, g, tb: (bi, g, 0, 0)),
        ],
        out_specs=(
            pl.BlockSpec((None, tblk, hb * hv), lambda bi, g, tb: (bi, tb, g)),
            pl.BlockSpec((None, hb, hk, hv), lambda bi, g, tb: (bi, g, 0, 0)),
        ),
        scratch_shapes=[pltpu.VMEM((hb, hv, hk), F32)],
        compiler_params=pltpu.CompilerParams(
            dimension_semantics=("arbitrary", "arbitrary", "arbitrary"), vmem_limit_bytes=_vmem(40)),
        name="hgrn_mixer",
    )(lb_logits, norm_gain.reshape(-1, 1, hv), u3, u3, u3, u3, s0)


def _sbp_kernel(bias_ref, q_ref, k_ref, v_ref, g_ref, o_ref, ko_ref, vo_ref, acc_ref, *, tq, tk, scale):
    h = pl.program_id(1)
    qi = pl.program_id(2)

    @pl.when(qi == 0)
    def _():
        ko_ref[...] = k_ref[...].astype(ko_ref.dtype)
        vo_ref[...] = v_ref[...].astype(vo_ref.dtype)

    bias = bias_ref[h]
    q = (q_ref[...] * scale).astype(BF16)
    row = lax.broadcasted_iota(jnp.int32, (tk, tk), 0)
    col = lax.broadcasted_iota(jnp.int32, (tk, tk), 1)
    suffix = jnp.where(row >= col, 1.0, 0.0).astype(BF16)
    suffix2 = jnp.concatenate([suffix, suffix], axis=0)
    acc_ref[...] = jnp.zeros_like(acc_ref)

    def pair(start, carry, r0):
        masked = r0 is not None
        r0 = r0 or 0
        rows = tq - r0
        kb = k_ref[pl.ds(start, 2 * tk), :].astype(BF16)
        vb = v_ref[pl.ds(start, 2 * tk), :].astype(BF16)
        z = lax.dot_general(q[r0:], kb, NT_DIMS, preferred_element_type=F32) + bias
        sp = _softplus(z)
        if masked:
            q_pos = qi * tq + r0 + lax.broadcasted_iota(jnp.int32, (rows, 2 * tk), 0)
            k_pos = start + lax.broadcasted_iota(jnp.int32, (rows, 2 * tk), 1)
            keep = k_pos < q_pos
            sp = jnp.where(keep, sp, 0.0)
        hi, lo = _split2(sp)
        rs_new = jnp.dot(jnp.concatenate([hi[:, tk:], lo[:, tk:]], axis=1), suffix2,
                         preferred_element_type=F32)
        rs_old = jnp.dot(jnp.concatenate([hi[:, :tk], lo[:, :tk]], axis=1), suffix2,
                         preferred_element_type=F32)
        carry_old = carry + rs_new[:, 0:1]
        w = jnp.concatenate([jnp.exp(z[:, :tk] - rs_old - carry_old),
                             jnp.exp(z[:, tk:] - rs_new - carry)], axis=1)
        if masked:
            w = jnp.where(keep, w, 0.0)
        acc_ref[r0:, :] += jnp.dot(w.astype(BF16), vb, preferred_element_type=F32)
        return carry_old + rs_old[:, 0:1]

    span = 2 * tk
    n_diag = tq // span
    carry = jnp.zeros((tq, 1), F32)
    for dj in range(n_diag):
        r0 = (n_diag - 1 - dj) * span
        c_new = pair(pl.multiple_of(qi * tq + r0, span), carry[r0:], r0)
        carry = c_new if r0 == 0 else jnp.concatenate([carry[:r0], c_new], axis=0)

    def body(jj, c):
        return pair(pl.multiple_of(qi * tq - (jj + 1) * span, span), c, None)

    lax.fori_loop(0, qi * n_diag, body, carry)
    o_ref[...] = (acc_ref[...] * _silu(g_ref[...])).astype(o_ref.dtype)


def _sbp_call(u3, bias, heads, hk, hv, row_dtypes):
    b, t, _ = u3.shape
    tq = min(t, SBP_TQ)
    tk = min(tq // 2, SBP_TK)
    qk = heads * hk
    kv = qk // hv
    kern = functools.partial(_sbp_kernel, tq=tq, tk=tk, scale=float(hk) ** -0.5)
    grid_spec = pltpu.PrefetchScalarGridSpec(
        num_scalar_prefetch=1,
        grid=(b, heads, t // tq),
        in_specs=[
            pl.BlockSpec((None, tq, hk), lambda bi, h, qi, bias: (bi, qi, h)),
            pl.BlockSpec((None, t, hk), lambda bi, h, qi, bias: (bi, 0, heads + h)),
            pl.BlockSpec((None, t, hv), lambda bi, h, qi, bias: (bi, 0, 2 * kv + h)),
            pl.BlockSpec((None, tq, hv), lambda bi, h, qi, bias: (bi, qi, 2 * kv + heads + h)),
        ],
        out_specs=(
            pl.BlockSpec((None, tq, hv), lambda bi, h, qi, bias: (bi, qi, h)),
            pl.BlockSpec((None, t, hk), lambda bi, h, qi, bias: (bi, 0, h)),
            pl.BlockSpec((None, t, hv), lambda bi, h, qi, bias: (bi, 0, h)),
        ),
        scratch_shapes=[pltpu.VMEM((tq, hv), F32)],
    )
    return pl.pallas_call(
        kern,
        out_shape=(jax.ShapeDtypeStruct((b, t, heads * hv), BF16),
                   jax.ShapeDtypeStruct((b, t, qk), row_dtypes[0]),
                   jax.ShapeDtypeStruct((b, t, heads * hv), row_dtypes[1])),
        grid_spec=grid_spec,
        compiler_params=pltpu.CompilerParams(
            dimension_semantics=("arbitrary", "arbitrary", "arbitrary"), vmem_limit_bytes=_vmem(56)),
        name="sb_prompt",
    )(bias, u3, u3, u3, u3)


def _sbs_kernel(pt_ref, bias_ref, wq_ref, kn_ref, vn_ref, *rest, pages_per_step, t_new, heads, hv,
                row_stride, row_start, scale):
    p = pages_per_step
    k_refs = rest[:p]
    v_refs = rest[p:3 * p]
    g_ref, o_ref, acc_ref, carry_ref = rest[3 * p:]
    s = pl.program_id(1)
    n_keys = kn_ref.shape[0]
    n_cols = heads * t_new

    row = lax.broadcasted_iota(jnp.int32, (n_keys, n_keys), 0)
    col = lax.broadcasted_iota(jnp.int32, (n_keys, n_keys), 1)
    newer = jnp.where(col > row, 1.0, 0.0).astype(BF16)
    newer2 = jnp.concatenate([newer, newer], axis=1)

    def page(kflat, vflat, mask):
        zt = jnp.dot(kflat, wq_ref[...], preferred_element_type=F32) * scale + bias_ref[...]
        sp = _softplus(zt)
        if mask is not None:
            sp = jnp.where(mask, sp, 0.0)
        hi, lo = _split2(sp)
        later = jnp.dot(newer2, jnp.concatenate([hi, lo], axis=0), preferred_element_type=F32)
        w = jnp.exp(zt - sp - later - carry_ref[...])
        if mask is not None:
            w = jnp.where(mask, w, 0.0)
        acc_ref[...] += lax.dot_general(w.astype(BF16), vflat, TN_DIMS,
                                        preferred_element_type=F32)
        carry_ref[...] += later[0:1, :] + sp[0:1, :]

    def head_rows(ref, h):
        return ref[pl.ds(row_start + h, n_keys, stride=row_stride), :]

    @pl.when(s == 0)
    def _():
        acc_ref[...] = jnp.zeros_like(acc_ref)
        carry_ref[...] = jnp.zeros_like(carry_ref)
        key_pos = lax.broadcasted_iota(jnp.int32, (n_keys, n_cols), 0)
        q_pos = lax.broadcasted_iota(jnp.int32, (n_keys, n_cols), 1) % t_new
        page(kn_ref[...].astype(BF16), vn_ref[...].astype(BF16), key_pos < q_pos)

    @pl.when(s > 0)
    def _():
        for i in range(p):
            kflat = jnp.concatenate([head_rows(k_refs[i], h).astype(BF16) for h in range(heads)], axis=1)
            vflat = jnp.concatenate(
                [head_rows(v_refs[2 * i + half], h).astype(BF16) for h in range(heads) for half in range(2)],
                axis=1)
            page(kflat, vflat, None)

    @pl.when(s == pl.num_programs(1) - 1)
    def _():
        for h in range(heads):
            o = acc_ref[h * t_new:(h + 1) * t_new, h * hv:(h + 1) * hv]
            o_ref[:, h * hv:(h + 1) * hv] = (o * _silu(g_ref[:, h * hv:(h + 1) * hv])).astype(o_ref.dtype)


def _sbs_call(us3, cache_k, cache_v, page_table, bias, slot, heads, hk, hv):
    db, t_new, _ = us3.shape
    n_pool, page, n_b = cache_k.shape[:3]
    n_pages = page_table.shape[1]
    qk = heads * hk
    br = heads * hv
    half = hv // 2
    p = 4
    while n_pages % p:
        p //= 2
    q = us3[..., :qk].reshape(db, t_new, heads, hk)
    eye = jnp.eye(heads, dtype=F32)
    wq = (q.transpose(0, 2, 3, 1)[:, :, :, None, :] * eye[None, :, None, :, None])
    wq = wq.reshape(db, qk, heads * t_new).astype(BF16)
    pad = ((0, 0), (0, page - t_new), (0, 0))
    k_new = jnp.pad(us3[..., qk:2 * qk], pad)
    v_new = jnp.pad(us3[..., 2 * qk:2 * qk + br], pad)
    g3 = us3[..., 2 * qk + br:]
    bias_row = jnp.repeat(bias.astype(F32), t_new).reshape(1, heads * t_new)
    rows = page * n_b * heads
    ck = cache_k.reshape(n_pool, rows, hk)
    cv = cache_v.reshape(n_pool, rows, hv)

    def page_map(i, lane_block):
        def index_map(b, s, pt):
            logical = n_pages - 1 - (jnp.maximum(s - 1, 0) * p + i)
            return (pt[b, logical], 0, lane_block)
        return index_map

    kern = functools.partial(_sbs_kernel, pages_per_step=p, t_new=t_new, heads=heads, hv=hv,
                             row_stride=n_b * heads, row_start=slot * heads, scale=float(hk) ** -0.5)
    grid_spec = pltpu.PrefetchScalarGridSpec(
        num_scalar_prefetch=1,
        grid=(db, 1 + n_pages // p),
        in_specs=(
            [pl.BlockSpec((1, heads * t_new), lambda b, s, pt: (0, 0)),
             pl.BlockSpec((None, qk, heads * t_new), lambda b, s, pt: (b, 0, 0)),
             pl.BlockSpec((None, page, qk), lambda b, s, pt: (b, 0, 0)),
             pl.BlockSpec((None, page, br), lambda b, s, pt: (b, 0, 0))]
            + [pl.BlockSpec((None, rows, hk), page_map(i, 0)) for i in range(p)]
            + [pl.BlockSpec((None, rows, half), page_map(i, lb)) for i in range(p) for lb in range(2)]
            + [pl.BlockSpec((None, t_new, br), lambda b, s, pt: (b, 0, 0))]),
        out_specs=pl.BlockSpec((None, t_new, br), lambda b, s, pt: (b, 0, 0)),
        scratch_shapes=[pltpu.VMEM((heads * t_new, br), F32), pltpu.VMEM((1, heads * t_new), F32)],
    )
    return pl.pallas_call(
        kern,
        out_shape=jax.ShapeDtypeStruct((db, t_new, br), BF16),
        grid_spec=grid_spec,
        compiler_params=pltpu.CompilerParams(
            dimension_semantics=("arbitrary", "arbitrary"), vmem_limit_bytes=_vmem(48)),
        name="sb_decode",
    )(page_table, bias_row, wq, k_new, v_new, *([ck] * p), *([cv] * (2 * p)), g3)


def kernel(x_prompt, x_sample, state_hgrn, cache_k, cache_v, page_table, c_prompt, c_sample, w_ada, b_ada,
           w_in, w_out, hgrn_lb_logits, hgrn_norm_gain, sb_logit_bias, ln_gain, ln_bias):
    b, t, d = x_prompt.shape
    db, dt, _ = x_sample.shape
    depth = w_in.shape[0]
    _, _, heads, hk, hv = state_hgrn.shape
    qk = heads * hk
    br = heads * hv
    in_width = w_in.shape[-1]
    alpha = (2.0 * depth) ** 0.25

    n_c = b + db
    c_all = jnp.pad(jnp.concatenate([c_prompt, c_sample], axis=0), ((0, (-n_c) % 8), (0, 0)))
    mods = _ada_call(c_all, w_ada, b_ada)
    w_in_bf = w_in.astype(BF16)
    w_out_bf = w_out.astype(BF16)

    tm_p = min(t, 1024)
    tm_o = min(t, 512)
    xp = x_prompt.reshape(b * t, d)
    xs = x_sample.reshape(db * dt, d)
    t_pad = -(-dt // HGRN_CHUNK) * HGRN_CHUNK
    hgrn_p, hgrn_s, kp_rows, vp_rows, ks_rows, vs_rows = [], [], [], [], [], []
    for layer in range(depth):
        slot = layer // N_MIXERS
        m = mods[layer]
        mp = [m[:b, i * d:(i + 1) * d].reshape(b, 1, d) for i in range(3)]
        ms = [jnp.repeat(m[b:n_c, i * d:(i + 1) * d], dt, axis=0).reshape(1, db * dt, d) for i in range(3)]
        up = _inproj_call(xp, mp[0], mp[1], w_in_bf, layer, t, tm_p)
        us = _inproj_call(xs, ms[0], ms[1], w_in_bf, layer, db * dt, db * dt)
        up3 = up.reshape(b, t, in_width)
        us3 = us.reshape(db, dt, in_width)
        if layer % N_MIXERS == 0:
            bp, sp = _hgrn_call(up3, hgrn_lb_logits, hgrn_norm_gain, jnp.zeros((b, heads, hk, hv), F32),
                                slot, heads, hk, hv)
            us_pad = jnp.pad(us3, ((0, 0), (0, t_pad - dt), (0, 0)))
            bs, ss = _hgrn_call(us_pad, hgrn_lb_logits, hgrn_norm_gain, state_hgrn[slot],
                                slot, heads, hk, hv, t_valid=dt)
            bs = bs[:, :dt]
            hgrn_p.append(sp.astype(state_hgrn.dtype))
            hgrn_s.append(ss.astype(state_hgrn.dtype))
        else:
            bias = sb_logit_bias[slot].astype(F32)
            bp, kp_flat, vp_flat = _sbp_call(up3, bias, heads, hk, hv, (cache_k.dtype, cache_v.dtype))
            bs = _sbs_call(us3, cache_k, cache_v, page_table, bias, slot, heads, hk, hv)
            kp_rows.append(kp_flat.reshape(b, t, heads, hk))
            vp_rows.append(vp_flat.reshape(b, t, heads, hv))
            ks_rows.append(us3[..., qk:2 * qk].reshape(db, dt, heads, hk).astype(cache_k.dtype))
            vs_rows.append(us3[..., 2 * qk:2 * qk + br].reshape(db, dt, heads, hv).astype(cache_v.dtype))
        xp = _outproj_call(xp, bp.reshape(b * t, br), w_out_bf, mp[2], ln_gain, ln_bias, layer, t, tm_o, alpha)
        xs = _outproj_call(xs, bs.reshape(db * dt, br), w_out_bf, ms[2], ln_gain, ln_bias, layer,
                           db * dt, db * dt, alpha)
    return (xp.reshape(b, t, d), xs.reshape(db, dt, d),
            jnp.stack(hgrn_p, axis=0), jnp.stack(hgrn_s, axis=0),
            jnp.stack(kp_rows, axis=2), jnp.stack(vp_rows, axis=2),
            jnp.stack(ks_rows, axis=2), jnp.stack(vs_rows, axis=2))
```

```python
import functools

import jax
import jax.numpy as jnp
from jax import lax
from jax.experimental import pallas as pl
from jax.experimental.pallas import tpu as pltpu

F32 = jnp.float32
BF16 = jnp.bfloat16

LN_EPS = 1e-5
RMS_EPS = 1e-6
HGRN_CHUNK = 64
HGRN_TBLK = 128
HGRN_HEADS_PER_STEP = 16
SBP_TQ = 1024
SBP_TK = 256
N_MIXERS = 2

NT_DIMS = (((1,), (1,)), ((), ()))
TN_DIMS = (((0,), (0,)), ((), ()))


def _vmem(mib):
    return mib * 1024 * 1024


def _sigmoid(x):
    return 1.0 / (1.0 + jnp.exp(-x))


def _silu(x):
    return x * _sigmoid(x)


def _softplus(z):
    return jnp.maximum(z, 0.0) + jnp.log(1.0 + jnp.exp(-jnp.abs(z)))


def _split2(x):
    hi = x.astype(BF16)
    lo = (x - hi.astype(F32)).astype(BF16)
    return hi, lo


def _split3(x):
    hi = x.astype(BF16)
    r = x - hi.astype(F32)
    mid = r.astype(BF16)
    lo = (r - mid.astype(F32)).astype(BF16)
    return hi, mid, lo


def _ada_kernel(c_ref, w_ref, b_ref, o_ref):
    o_ref[...] = jnp.dot(c_ref[...], w_ref[...], precision=lax.Precision.HIGHEST,
                         preferred_element_type=F32) + b_ref[...]


def _ada_call(c_all, w_ada, b_ada):
    depth, d, n = w_ada.shape
    rows = c_all.shape[0]
    tn = 1024
    return pl.pallas_call(
        _ada_kernel,
        out_shape=jax.ShapeDtypeStruct((depth, rows, n), F32),
        grid=(depth, n // tn),
        in_specs=[
            pl.BlockSpec((rows, d), lambda l, j: (0, 0)),
            pl.BlockSpec((None, d, tn), lambda l, j: (l, 0, j)),
            pl.BlockSpec((None, 1, tn), lambda l, j: (l, 0, j)),
        ],
        out_specs=pl.BlockSpec((None, rows, tn), lambda l, j: (l, 0, j)),
        compiler_params=pltpu.CompilerParams(
            dimension_semantics=("arbitrary", "arbitrary"), vmem_limit_bytes=_vmem(40)),
        name="ada_mod",
    )(c_all, w_ada, b_ada.reshape(depth, 1, n))


def _inproj_kernel(x_ref, shift_ref, scale_ref, w_ref, u_ref, h_ref):
    @pl.when(pl.program_id(1) == 0)
    def _():
        h_ref[...] = (x_ref[...] * (1.0 + scale_ref[...]) + shift_ref[...]).astype(BF16)

    u_ref[...] = jnp.dot(h_ref[...], w_ref[...], preferred_element_type=F32)


def _inproj_call(x2, shift, scale, w_bf, layer, rows_per_mod, tm):
    m, d = x2.shape
    n = w_bf.shape[-1]
    tn = 1024
    r = shift.shape[1]
    tiles_per_mod = rows_per_mod // tm
    mod_spec = pl.BlockSpec((None, r, d), lambda i, j: (i // tiles_per_mod, 0, 0))
    return pl.pallas_call(
        _inproj_kernel,
        out_shape=jax.ShapeDtypeStruct((m, n), F32),
        grid=(m // tm, n // tn),
        in_specs=[
            pl.BlockSpec((tm, d), lambda i, j: (i, 0)),
            mod_spec, mod_spec,
            pl.BlockSpec((None, d, tn), lambda i, j: (layer, 0, j)),
        ],
        out_specs=pl.BlockSpec((tm, tn), lambda i, j: (i, j)),
        scratch_shapes=[pltpu.VMEM((tm, d), BF16)],
        compiler_params=pltpu.CompilerParams(
            dimension_semantics=("arbitrary", "arbitrary"), vmem_limit_bytes=_vmem(48)),
        name="in_proj",
    )(x2, shift, scale, w_bf)


def _outproj_kernel(x_ref, br_ref, w_ref, gate_ref, lng_ref, lnb_ref, y_ref, acc_ref, *, alpha, nk):
    k = pl.program_id(1)

    @pl.when(k == 0)
    def _():
        acc_ref[...] = jnp.zeros_like(acc_ref)

    acc_ref[...] += jnp.dot(br_ref[...], w_ref[...], preferred_element_type=F32)

    @pl.when(k == nk - 1)
    def _():
        v = alpha * x_ref[...] + gate_ref[...] * acc_ref[...]
        mu = jnp.mean(v, axis=-1, keepdims=True)
        dv = v - mu
        var = jnp.mean(dv * dv, axis=-1, keepdims=True)
        y_ref[...] = dv * lax.rsqrt(var + LN_EPS) * lng_ref[...] + lnb_ref[...]


def _outproj_call(x2, br2, w_bf, gate, ln_gain, ln_bias, layer, rows_per_mod, tm, alpha):
    m, d = x2.shape
    kdim = br2.shape[1]
    tk = 2048
    nk = kdim // tk
    r = gate.shape[1]
    tiles_per_mod = rows_per_mod // tm
    depth = ln_gain.shape[0]
    vec_spec = pl.BlockSpec((None, 1, d), lambda i, k: (layer, 0, 0))
    return pl.pallas_call(
        functools.partial(_outproj_kernel, alpha=alpha, nk=nk),
        out_shape=jax.ShapeDtypeStruct((m, d), F32),
        grid=(m // tm, nk),
        in_specs=[
            pl.BlockSpec((tm, d), lambda i, k: (i, 0)),
            pl.BlockSpec((tm, tk), lambda i, k: (i, k)),
            pl.BlockSpec((None, tk, d), lambda i, k: (layer, k, 0)),
            pl.BlockSpec((None, r, d), lambda i, k: (i // tiles_per_mod, 0, 0)),
            vec_spec, vec_spec,
        ],
        out_specs=pl.BlockSpec((tm, d), lambda i, k: (i, 0)),
        scratch_shapes=[pltpu.VMEM((tm, d), F32)],
        compiler_params=pltpu.CompilerParams(
            dimension_semantics=("arbitrary", "arbitrary"), vmem_limit_bytes=_vmem(48)),
        name="out_proj_ln",
    )(x2, br2, w_bf, gate, ln_gain.reshape(depth, 1, d), ln_bias.reshape(depth, 1, d))


def _hgrn_kernel(lbl_ref, gain_ref, q_ref, f_ref, i_ref, g_ref, s0_ref, o_ref, s_ref, st_ref, *,
                 slot, chunk, n_chunks, n_heads, hk, hv, t_valid):
    tb = pl.program_id(2)
    c = chunk

    @pl.when(tb == 0)
    def _():
        for hh in range(n_heads):
            st_ref[hh] = s0_ref[hh].T

    lg = lbl_ref[...]
    e = jnp.exp(lg - jnp.max(lg, axis=0, keepdims=True))
    lb_all = jnp.sum(e[:slot + 1], axis=0, keepdims=True) / jnp.sum(e, axis=0, keepdims=True)
    gain = gain_ref[...]

    row = lax.broadcasted_iota(jnp.int32, (c, c), 0)
    col = lax.broadcasted_iota(jnp.int32, (c, c), 1)
    causal = col <= row
    tri = jnp.where(causal, 1.0, 0.0).astype(BF16)
    tri3 = jnp.concatenate([tri, tri, tri], axis=1)

    gain_all = jnp.concatenate([gain] * n_heads, axis=1)
    for ci in range(n_chunks):
        sl = pl.ds(ci * c, c)
        zf = f_ref[sl, :]
        t = jnp.exp(-jnp.abs(zf))
        r = 1.0 / (1.0 + t)
        pos = zf >= 0.0
        sig = jnp.where(pos, r, t * r)
        nsig = jnp.where(pos, t * r, r)
        log_f = jnp.log(lb_all + (1.0 - lb_all) * sig)
        k = (1.0 - lb_all) * nsig
        if t_valid is not None:
            tpos = (tb * n_chunks + ci) * c + lax.broadcasted_iota(jnp.int32, (c, 1), 0)
            valid = tpos < t_valid
            log_f = jnp.where(valid, log_f, 0.0)
            k = jnp.where(valid, k, 0.0)
        cum = jnp.dot(tri3, jnp.concatenate(_split3(log_f), axis=0),
                      preferred_element_type=F32)
        ref = cum[c // 2:c // 2 + 1, :]
        last = cum[c - 1:c, :]
        q = _silu(q_ref[sl, :])
        qe = (q * jnp.exp(cum - ref)).astype(BF16)
        ke = (k * jnp.exp(ref - cum)).astype(BF16)
        qd = (q * jnp.exp(cum)).astype(BF16)
        kd = (k * jnp.exp(last - cum)).astype(BF16)
        decay = jnp.exp(last)
        vb_all = i_ref[sl, :].astype(BF16)
        normed = []
        for hh in range(n_heads):
            ks = slice(hh * hk, (hh + 1) * hk)
            vb = vb_all[:, hh * hv:(hh + 1) * hv]
            scores = lax.dot_general(qe[:, ks], ke[:, ks], NT_DIMS, preferred_element_type=F32)
            scores = jnp.where(causal, scores, 0.0).astype(BF16)
            st = st_ref[hh]
            o = (jnp.dot(scores, vb, preferred_element_type=F32)
                 + lax.dot_general(qd[:, ks], st.astype(BF16), NT_DIMS, preferred_element_type=F32))
            st_ref[hh] = decay[:, ks] * st + lax.dot_general(vb, kd[:, ks], TN_DIMS,
                                                            preferred_element_type=F32)
            ms = jnp.mean(o * o, axis=-1, keepdims=True)
            normed.append(o * lax.rsqrt(ms + RMS_EPS))
        on = jnp.concatenate(normed, axis=1)
        o_ref[sl, :] = (on * gain_all * _silu(g_ref[sl, :])).astype(o_ref.dtype)

    @pl.when(tb == pl.num_programs(2) - 1)
    def _():
        for hh in range(n_heads):
            s_ref[hh] = st_ref[hh].T


def _hgrn_call(u3, lb_logits, norm_gain, s0, slot, heads, hk, hv, t_valid=None):
    b, t, _ = u3.shape
    tblk = min(t, HGRN_TBLK)
    n_chunks = tblk // HGRN_CHUNK
    hb = HGRN_HEADS_PER_STEP
    groups = heads // hb
    qk = heads * hk
    v_off = 2 * qk // (hb * hv)
    n_lb = lb_logits.shape[0]
    kern = functools.partial(_hgrn_kernel, slot=slot, chunk=HGRN_CHUNK, n_chunks=n_chunks,
                             n_heads=hb, hk=hk, hv=hv, t_valid=t_valid)
    return pl.pallas_call(
        kern,
        out_shape=(jax.ShapeDtypeStruct((b, t, heads * hv), BF16),
                   jax.ShapeDtypeStruct((b, heads, hk, hv), F32)),
        grid=(b, groups, t // tblk),
        in_specs=[
            pl.BlockSpec((n_lb, hb * hk), lambda bi, g, tb: (0, g)),
            pl.BlockSpec((None, 1, hv), lambda bi, g, tb: (slot, 0, 0)),
            pl.BlockSpec((None, tblk, hb * hk), lambda bi, g, tb: (bi, tb, g)),
            pl.BlockSpec((None, tblk, hb * hk), lambda bi, g, tb: (bi, tb, groups + g)),
            pl.BlockSpec((None, tblk, hb * hv), lambda bi, g, tb: (bi, tb, v_off + g)),
            pl.BlockSpec((None, tblk, hb * hv), lambda bi, g, tb: (bi, tb, v_off + groups + g)),
            pl.BlockSpec((None, hb, hk, hv), lambda bi, g, tb: (bi, g, 0, 0)),
        ],
        out_specs=(
            pl.BlockSpec((None, tblk, hb * hv), lambda bi, g, tb: (bi, tb, g)),
            pl.BlockSpec((None, hb, hk, hv), lambda bi, g, tb: (bi, g, 0, 0)),
        ),
        scratch_shapes=[pltpu.VMEM((hb, hv, hk), F32)],
        compiler_params=pltpu.CompilerParams(
            dimension_semantics=("arbitrary", "arbitrary", "arbitrary"), vmem_limit_bytes=_vmem(40)),
        name="hgrn_mixer",
    )(lb_logits, norm_gain.reshape(-1, 1, hv), u3, u3, u3, u3, s0)


def _sbp_kernel(bias_ref, q_ref, k_ref, v_ref, g_ref, o_ref, ko_ref, vo_ref, acc_ref, *, tq, tk, scale):
    h = pl.program_id(1)
    qi = pl.program_id(2)

    @pl.when(qi == 0)
    def _():
        ko_ref[...] = k_ref[...].astype(ko_ref.dtype)
        vo_ref[...] = v_ref[...].astype(vo_ref.dtype)

    bias = bias_ref[h]
    q = (q_ref[...] * scale).astype(BF16)
    row = lax.broadcasted_iota(jnp.int32, (tk, tk), 0)
    col = lax.broadcasted_iota(jnp.int32, (tk, tk), 1)
    suffix = jnp.where(row >= col, 1.0, 0.0).astype(BF16)
    suffix2 = jnp.concatenate([suffix, suffix], axis=0)
    acc_ref[...] = jnp.zeros_like(acc_ref)

    def pair(start, carry, r0):
        masked = r0 is not None
        r0 = r0 or 0
        rows = tq - r0
        kb = k_ref[pl.ds(start, 2 * tk), :].astype(BF16)
        vb = v_ref[pl.ds(start, 2 * tk), :].astype(BF16)
        z = lax.dot_general(q[r0:], kb, NT_DIMS, preferred_element_type=F32) + bias
        sp = _softplus(z)
        if masked:
            q_pos = qi * tq + r0 + lax.broadcasted_iota(jnp.int32, (rows, 2 * tk), 0)
            k_pos = start + lax.broadcasted_iota(jnp.int32, (rows, 2 * tk), 1)
            keep = k_pos < q_pos
            sp = jnp.where(keep, sp, 0.0)
        hi, lo = _split2(sp)
        rs_new = jnp.dot(jnp.concatenate([hi[:, tk:], lo[:, tk:]], axis=1), suffix2,
                         preferred_element_type=F32)
        rs_old = jnp.dot(jnp.concatenate([hi[:, :tk], lo[:, :tk]], axis=1), suffix2,
                         preferred_element_type=F32)
        carry_old = carry + rs_new[:, 0:1]
        w = jnp.concatenate([jnp.exp(z[:, :tk] - rs_old - carry_old),
                             jnp.exp(z[:, tk:] - rs_new - carry)], axis=1)
        if masked:
            w = jnp.where(keep, w, 0.0)
        acc_ref[r0:, :] += jnp.dot(w.astype(BF16), vb, preferred_element_type=F32)
        return carry_old + rs_old[:, 0:1]

    span = 2 * tk
    n_diag = tq // span
    carry = jnp.zeros((tq, 1), F32)
    for dj in range(n_diag):
        r0 = (n_diag - 1 - dj) * span
        c_new = pair(pl.multiple_of(qi * tq + r0, span), carry[r0:], r0)
        carry = c_new if r0 == 0 else jnp.concatenate([carry[:r0], c_new], axis=0)

    def body(jj, c):
        return pair(pl.multiple_of(qi * tq - (jj + 1) * span, span), c, None)

    lax.fori_loop(0, qi * n_diag, body, carry)
    o_ref[...] = (acc_ref[...] * _silu(g_ref[...])).astype(o_ref.dtype)


def _sbp_call(u3, bias, heads, hk, hv, row_dtypes):
    b, t, _ = u3.shape
    tq = min(t, SBP_TQ)
    tk = min(tq // 2, SBP_TK)
    qk = heads * hk
    kv = qk // hv
    kern = functools.partial(_sbp_kernel, tq=tq, tk=tk, scale=float(hk) ** -0.5)
    grid_spec = pltpu.PrefetchScalarGridSpec(
        num_scalar_prefetch=1,
        grid=(b, heads, t // tq),
        in_specs=[
            pl.BlockSpec((None, tq, hk), lambda bi, h, qi, bias: (bi, qi, h)),
            pl.BlockSpec((None, t, hk), lambda bi, h, qi, bias: (bi, 0, heads + h)),
            pl.BlockSpec((None, t, hv), lambda bi, h, qi, bias: (bi, 0, 2 * kv + h)),
            pl.BlockSpec((None, tq, hv), lambda bi, h, qi, bias: (bi, qi, 2 * kv + heads + h)),
        ],
        out_specs=(
            pl.BlockSpec((None, tq, hv), lambda bi, h, qi, bias: (bi, qi, h)),
            pl.BlockSpec((None, t, hk), lambda bi, h, qi, bias: (bi, 0, h)),
            pl.BlockSpec((None, t, hv), lambda bi, h, qi, bias: (bi, 0, h)),
        ),
        scratch_shapes=[pltpu.VMEM((tq, hv), F32)],
    )
    return pl.pallas_call(
        kern,
        out_shape=(jax.ShapeDtypeStruct((b, t, heads * hv), BF16),
                   jax.ShapeDtypeStruct((b, t, qk), row_dtypes[0]),
                   jax.ShapeDtypeStruct((b, t, heads * hv), row_dtypes[1])),
        grid_spec=grid_spec,
        compiler_params=pltpu.CompilerParams(
            dimension_semantics=("arbitrary", "arbitrary", "arbitrary"), vmem_limit_bytes=_vmem(56)),
        name="sb_prompt",
    )(bias, u3, u3, u3, u3)


def _sbs_kernel(pt_ref, bias_ref, wq_ref, kn_ref, vn_ref, *rest, pages_per_step, t_new, heads, hv,
                row_stride, row_start, scale):
    p = pages_per_step
    k_refs = rest[:p]
    v_refs = rest[p:2 * p]
    g_ref, o_ref, acc_ref, carry_ref, wbuf_ref = rest[2 * p:]
    s = pl.program_id(1)
    n_keys = kn_ref.shape[0]
    n_cols = wq_ref.shape[1]

    row = lax.broadcasted_iota(jnp.int32, (n_keys, n_keys), 0)
    col = lax.broadcasted_iota(jnp.int32, (n_keys, n_keys), 1)
    newer = jnp.where(col > row, 1.0, 0.0).astype(BF16)
    newer2 = jnp.concatenate([newer, newer], axis=1)
    hsel_h = lax.broadcasted_iota(jnp.int32, (row_stride, n_cols), 0) - row_start
    hsel_c = lax.broadcasted_iota(jnp.int32, (row_stride, n_cols), 1)
    head_sel = jnp.where(jnp.logical_and(hsel_c // t_new == hsel_h, hsel_c < heads * t_new), 1.0, 0.0)

    def weights(kflat, mask):
        zt = jnp.dot(kflat, wq_ref[...], preferred_element_type=F32) * scale + bias_ref[...]
        sp = _softplus(zt)
        if mask is not None:
            sp = jnp.where(mask, sp, 0.0)
        hi, lo = _split2(sp)
        later = jnp.dot(newer2, jnp.concatenate([hi, lo], axis=0), preferred_element_type=F32)
        w = jnp.exp(zt - sp - later - carry_ref[...])
        if mask is not None:
            w = jnp.where(mask, w, 0.0)
        carry_ref[...] += later[0:1, :] + sp[0:1, :]
        return w

    def head_rows(ref, h):
        return ref[pl.ds(row_start + h, n_keys, stride=row_stride), :]

    @pl.when(s == 0)
    def _():
        acc_ref[...] = jnp.zeros_like(acc_ref)
        carry_ref[...] = jnp.zeros_like(carry_ref)
        key_pos = lax.broadcasted_iota(jnp.int32, (n_keys, n_cols), 0)
        q_pos = lax.broadcasted_iota(jnp.int32, (n_keys, n_cols), 1) % t_new
        w = weights(kn_ref[...].astype(BF16), key_pos < q_pos)
        for h in range(heads):
            wh = (w * head_sel[row_start + h:row_start + h + 1, :]).astype(BF16)
            acc_ref[...] += lax.dot_general(wh, vn_ref[:, h * hv:(h + 1) * hv].astype(BF16), TN_DIMS,
                                            preferred_element_type=F32)

    @pl.when(s > 0)
    def _():
        for i in range(p):
            kflat = jnp.concatenate([head_rows(k_refs[i], h).astype(BF16) for h in range(heads)], axis=1)
            wbuf_ref[...] = weights(kflat, None)
            wexp = jnp.concatenate(
                [jnp.broadcast_to(wbuf_ref[key:key + 1, :], (row_stride, n_cols)) * head_sel
                 for key in range(n_keys)], axis=0).astype(BF16)
            acc_ref[...] += lax.dot_general(wexp, v_refs[i][...].astype(BF16), TN_DIMS,
                                            preferred_element_type=F32)

    @pl.when(s == pl.num_programs(1) - 1)
    def _():
        for h in range(heads):
            o = acc_ref[h * t_new:(h + 1) * t_new, :]
            o_ref[:, h * hv:(h + 1) * hv] = (o * _silu(g_ref[:, h * hv:(h + 1) * hv])).astype(o_ref.dtype)


def _sbs_call(us3, cache_k, cache_v, page_table, bias, slot, heads, hk, hv):
    db, t_new, _ = us3.shape
    n_pool, page, n_b = cache_k.shape[:3]
    n_pages = page_table.shape[1]
    qk = heads * hk
    br = heads * hv
    n_cols = -(-heads * t_new // 128) * 128
    p = 4
    while n_pages % p:
        p //= 2
    q = us3[..., :qk].reshape(db, t_new, heads, hk)
    eye = jnp.eye(heads, dtype=F32)
    wq = (q.transpose(0, 2, 3, 1)[:, :, :, None, :] * eye[None, :, None, :, None])
    wq = wq.reshape(db, qk, heads * t_new)
    wq = jnp.pad(wq, ((0, 0), (0, 0), (0, n_cols - heads * t_new))).astype(BF16)
    pad = ((0, 0), (0, page - t_new), (0, 0))
    k_new = jnp.pad(us3[..., qk:2 * qk], pad)
    v_new = jnp.pad(us3[..., 2 * qk:2 * qk + br], pad)
    g3 = us3[..., 2 * qk + br:]
    bias_row = jnp.pad(jnp.repeat(bias.astype(F32), t_new), (0, n_cols - heads * t_new)).reshape(1, n_cols)
    rows = page * n_b * heads
    ck = cache_k.reshape(n_pool, rows, hk)
    cv = cache_v.reshape(n_pool, rows, hv)

    def page_map(i):
        def index_map(b, s, pt):
            logical = n_pages - 1 - (jnp.maximum(s - 1, 0) * p + i)
            return (pt[b, logical], 0, 0)
        return index_map

    kern = functools.partial(_sbs_kernel, pages_per_step=p, t_new=t_new, heads=heads, hv=hv,
                             row_stride=n_b * heads, row_start=slot * heads, scale=float(hk) ** -0.5)
    grid_spec = pltpu.PrefetchScalarGridSpec(
        num_scalar_prefetch=1,
        grid=(db, 1 + n_pages // p),
        in_specs=(
            [pl.BlockSpec((1, n_cols), lambda b, s, pt: (0, 0)),
             pl.BlockSpec((None, qk, n_cols), lambda b, s, pt: (b, 0, 0)),
             pl.BlockSpec((None, page, qk), lambda b, s, pt: (b, 0, 0)),
             pl.BlockSpec((None, page, br), lambda b, s, pt: (b, 0, 0))]
            + [pl.BlockSpec((None, rows, hk), page_map(i)) for i in range(p)]
            + [pl.BlockSpec((None, rows, hv), page_map(i)) for i in range(p)]
            + [pl.BlockSpec((None, t_new, br), lambda b, s, pt: (b, 0, 0))]),
        out_specs=pl.BlockSpec((None, t_new, br), lambda b, s, pt: (b, 0, 0)),
        scratch_shapes=[pltpu.VMEM((n_cols, hv), F32), pltpu.VMEM((1, n_cols), F32),
                        pltpu.VMEM((page, n_cols), F32)],
    )
    return pl.pallas_call(
        kern,
        out_shape=jax.ShapeDtypeStruct((db, t_new, br), BF16),
        grid_spec=grid_spec,
        compiler_params=pltpu.CompilerParams(
            dimension_semantics=("arbitrary", "arbitrary"), vmem_limit_bytes=_vmem(48)),
        name="sb_decode",
    )(page_table, bias_row, wq, k_new, v_new, *([ck] * p), *([cv] * p), g3)


def kernel(x_prompt, x_sample, state_hgrn, cache_k, cache_v, page_table, c_prompt, c_sample, w_ada, b_ada,
           w_in, w_out, hgrn_lb_logits, hgrn_norm_gain, sb_logit_bias, ln_gain, ln_bias):
    b, t, d = x_prompt.shape
    db, dt, _ = x_sample.shape
    depth = w_in.shape[0]
    _, _, heads, hk, hv = state_hgrn.shape
    qk = heads * hk
    br = heads * hv
    in_width = w_in.shape[-1]
    alpha = (2.0 * depth) ** 0.25

    n_c = b + db
    c_all = jnp.pad(jnp.concatenate([c_prompt, c_sample], axis=0), ((0, (-n_c) % 8), (0, 0)))
    mods = _ada_call(c_all, w_ada, b_ada)
    w_in_bf = w_in.astype(BF16)
    w_out_bf = w_out.astype(BF16)

    tm_p = min(t, 1024)
    tm_o = min(t, 512)
    xp = x_prompt.reshape(b * t, d)
    xs = x_sample.reshape(db * dt, d)
    t_pad = -(-dt // HGRN_CHUNK) * HGRN_CHUNK
    hgrn_p, hgrn_s, kp_rows, vp_rows, ks_rows, vs_rows = [], [], [], [], [], []
    for layer in range(depth):
        slot = layer // N_MIXERS
        m = mods[layer]
        mp = [m[:b, i * d:(i + 1) * d].reshape(b, 1, d) for i in range(3)]
        ms = [jnp.repeat(m[b:n_c, i * d:(i + 1) * d], dt, axis=0).reshape(1, db * dt, d) for i in range(3)]
        up = _inproj_call(xp, mp[0], mp[1], w_in_bf, layer, t, tm_p)
        us = _inproj_call(xs, ms[0], ms[1], w_in_bf, layer, db * dt, db * dt)
        up3 = up.reshape(b, t, in_width)
        us3 = us.reshape(db, dt, in_width)
        if layer % N_MIXERS == 0:
            bp, sp = _hgrn_call(up3, hgrn_lb_logits, hgrn_norm_gain, jnp.zeros((b, heads, hk, hv), F32),
                                slot, heads, hk, hv)
            us_pad = jnp.pad(us3, ((0, 0), (0, t_pad - dt), (0, 0)))
            bs, ss = _hgrn_call(us_pad, hgrn_lb_logits, hgrn_norm_gain, state_hgrn[slot],
                                slot, heads, hk, hv, t_valid=dt)
            bs = bs[:, :dt]
            hgrn_p.append(sp.astype(state_hgrn.dtype))
            hgrn_s.append(ss.astype(state_hgrn.dtype))
        else:
            bias = sb_logit_bias[slot].astype(F32)
            bp, kp_flat, vp_flat = _sbp_call(up3, bias, heads, hk, hv, (cache_k.dtype, cache_v.dtype))
            bs = _sbs_call(us3, cache_k, cache_v, page_table, bias, slot, heads, hk, hv)
            kp_rows.append(kp_flat.reshape(b, t, heads, hk))
            vp_rows.append(vp_flat.reshape(b, t, heads, hv))
            ks_rows.append(us3[..., qk:2 * qk].reshape(db, dt, heads, hk).astype(cache_k.dtype))
            vs_rows.append(us3[..., 2 * qk:2 * qk + br].reshape(db, dt, heads, hv).astype(cache_v.dtype))
        xp = _outproj_call(xp, bp.reshape(b * t, br), w_out_bf, mp[2], ln_gain, ln_bias, layer, t, tm_o, alpha)
        xs = _outproj_call(xs, bs.reshape(db * dt, br), w_out_bf, ms[2], ln_gain, ln_bias, layer,
                           db * dt, db * dt, alpha)
    return (xp.reshape(b, t, d), xs.reshape(db, dt, d),
            jnp.stack(hgrn_p, axis=0), jnp.stack(hgrn_s, axis=0),
            jnp.stack(kp_rows, axis=2), jnp.stack(vp_rows, axis=2),
            jnp.stack(ks_rows, axis=2), jnp.stack(vs_rows, axis=2))
```

```python
import functools

import jax
import jax.numpy as jnp
from jax import lax
from jax.experimental import pallas as pl
from jax.experimental.pallas import tpu as pltpu

F32 = jnp.float32
BF16 = jnp.bfloat16

LN_EPS = 1e-5
RMS_EPS = 1e-6
HGRN_CHUNK = 64
HGRN_TBLK = 128
HGRN_HEADS_PER_STEP = 16
SBP_TQ = 1024
SBP_TK = 256
N_MIXERS = 2

NT_DIMS = (((1,), (1,)), ((), ()))
TN_DIMS = (((0,), (0,)), ((), ()))


def _vmem(mib):
    return mib * 1024 * 1024


def _sigmoid(x):
    return 1.0 / (1.0 + jnp.exp(-x))


def _silu(x):
    return x * _sigmoid(x)


def _softplus(z):
    return jnp.maximum(z, 0.0) + jnp.log(1.0 + jnp.exp(-jnp.abs(z)))


def _split2(x):
    hi = x.astype(BF16)
    lo = (x - hi.astype(F32)).astype(BF16)
    return hi, lo


def _split3(x):
    hi = x.astype(BF16)
    r = x - hi.astype(F32)
    mid = r.astype(BF16)
    lo = (r - mid.astype(F32)).astype(BF16)
    return hi, mid, lo


def _ada_kernel(c_ref, w_ref, b_ref, o_ref):
    o_ref[...] = jnp.dot(c_ref[...], w_ref[...], precision=lax.Precision.HIGHEST,
                         preferred_element_type=F32) + b_ref[...]


def _ada_call(c_all, w_ada, b_ada):
    depth, d, n = w_ada.shape
    rows = c_all.shape[0]
    tn = 1024
    return pl.pallas_call(
        _ada_kernel,
        out_shape=jax.ShapeDtypeStruct((depth, rows, n), F32),
        grid=(depth, n // tn),
        in_specs=[
            pl.BlockSpec((rows, d), lambda l, j: (0, 0)),
            pl.BlockSpec((None, d, tn), lambda l, j: (l, 0, j)),
            pl.BlockSpec((None, 1, tn), lambda l, j: (l, 0, j)),
        ],
        out_specs=pl.BlockSpec((None, rows, tn), lambda l, j: (l, 0, j)),
        compiler_params=pltpu.CompilerParams(
            dimension_semantics=("arbitrary", "arbitrary"), vmem_limit_bytes=_vmem(40)),
        name="ada_mod",
    )(c_all, w_ada, b_ada.reshape(depth, 1, n))


def _inproj_kernel(x_ref, shift_ref, scale_ref, xs_ref, shift_s_ref, scale_s_ref, w_ref, u_ref, us_ref, h_ref,
                   *, tm):
    @pl.when(pl.program_id(1) == 0)
    def _():
        h_ref[:tm, :] = (x_ref[...] * (1.0 + scale_ref[...]) + shift_ref[...]).astype(BF16)
        h_ref[tm:, :] = (xs_ref[...] * (1.0 + scale_s_ref[...]) + shift_s_ref[...]).astype(BF16)

    u = jnp.dot(h_ref[...], w_ref[...], preferred_element_type=F32)
    u_ref[...] = u[:tm]
    us_ref[...] = u[tm:]


def _inproj_call(x2, shift, scale, xs2, shift_s, scale_s, w_bf, layer, rows_per_mod, tm):
    m, d = x2.shape
    ms = xs2.shape[0]
    n = w_bf.shape[-1]
    tn = 1024
    tiles_per_mod = rows_per_mod // tm
    mod_spec = pl.BlockSpec((None, 1, d), lambda i, j: (i // tiles_per_mod, 0, 0))
    s_spec = pl.BlockSpec((ms, d), lambda i, j: (0, 0))
    return pl.pallas_call(
        functools.partial(_inproj_kernel, tm=tm),
        out_shape=(jax.ShapeDtypeStruct((m, n), F32), jax.ShapeDtypeStruct((m // tm, ms, n), F32)),
        grid=(m // tm, n // tn),
        in_specs=[
            pl.BlockSpec((tm, d), lambda i, j: (i, 0)),
            mod_spec, mod_spec,
            s_spec, s_spec, s_spec,
            pl.BlockSpec((None, d, tn), lambda i, j: (layer, 0, j)),
        ],
        out_specs=(pl.BlockSpec((tm, tn), lambda i, j: (i, j)),
                   pl.BlockSpec((None, ms, tn), lambda i, j: (i, 0, j))),
        scratch_shapes=[pltpu.VMEM((tm + ms, d), BF16)],
        compiler_params=pltpu.CompilerParams(
            dimension_semantics=("arbitrary", "arbitrary"), vmem_limit_bytes=_vmem(48)),
        name="in_proj",
    )(x2, shift, scale, xs2, shift_s, scale_s, w_bf)


def _outproj_kernel(x_ref, br_ref, w_ref, gate_ref, lng_ref, lnb_ref, y_ref, acc_ref, *, alpha, nk):
    k = pl.program_id(1)

    @pl.when(k == 0)
    def _():
        acc_ref[...] = jnp.zeros_like(acc_ref)

    acc_ref[...] += jnp.dot(br_ref[...], w_ref[...], preferred_element_type=F32)

    @pl.when(k == nk - 1)
    def _():
        v = alpha * x_ref[...] + gate_ref[...] * acc_ref[...]
        mu = jnp.mean(v, axis=-1, keepdims=True)
        dv = v - mu
        var = jnp.mean(dv * dv, axis=-1, keepdims=True)
        y_ref[...] = dv * lax.rsqrt(var + LN_EPS) * lng_ref[...] + lnb_ref[...]


def _outproj_call(x2, br2, w_bf, gate, ln_gain, ln_bias, layer, rows_per_mod, tm, alpha):
    m, d = x2.shape
    kdim = br2.shape[1]
    tk = 2048
    nk = kdim // tk
    r = gate.shape[1]
    tiles_per_mod = rows_per_mod // tm
    depth = ln_gain.shape[0]
    vec_spec = pl.BlockSpec((None, 1, d), lambda i, k: (layer, 0, 0))
    return pl.pallas_call(
        functools.partial(_outproj_kernel, alpha=alpha, nk=nk),
        out_shape=jax.ShapeDtypeStruct((m, d), F32),
        grid=(m // tm, nk),
        in_specs=[
            pl.BlockSpec((tm, d), lambda i, k: (i, 0)),
            pl.BlockSpec((tm, tk), lambda i, k: (i, k)),
            pl.BlockSpec((None, tk, d), lambda i, k: (layer, k, 0)),
            pl.BlockSpec((None, r, d), lambda i, k: (i // tiles_per_mod, 0, 0)),
            vec_spec, vec_spec,
        ],
        out_specs=pl.BlockSpec((tm, d), lambda i, k: (i, 0)),
        scratch_shapes=[pltpu.VMEM((tm, d), F32)],
        compiler_params=pltpu.CompilerParams(
            dimension_semantics=("arbitrary", "arbitrary"), vmem_limit_bytes=_vmem(48)),
        name="out_proj_ln",
    )(x2, br2, w_bf, gate, ln_gain.reshape(depth, 1, d), ln_bias.reshape(depth, 1, d))


def _hgrn_kernel(lbl_ref, gain_ref, q_ref, f_ref, i_ref, g_ref, s0_ref, o_ref, s_ref, st_ref, *,
                 slot, chunk, n_chunks, n_heads, hk, hv, t_valid):
    tb = pl.program_id(2)
    c = chunk

    @pl.when(tb == 0)
    def _():
        for hh in range(n_heads):
            st_ref[hh] = s0_ref[hh].T

    lg = lbl_ref[...]
    e = jnp.exp(lg - jnp.max(lg, axis=0, keepdims=True))
    lb_all = jnp.sum(e[:slot + 1], axis=0, keepdims=True) / jnp.sum(e, axis=0, keepdims=True)
    gain = gain_ref[...]

    row = lax.broadcasted_iota(jnp.int32, (c, c), 0)
    col = lax.broadcasted_iota(jnp.int32, (c, c), 1)
    causal = col <= row
    tri = jnp.where(causal, 1.0, 0.0).astype(BF16)
    tri3 = jnp.concatenate([tri, tri, tri], axis=1)

    gain_all = jnp.concatenate([gain] * n_heads, axis=1)
    for ci in range(n_chunks):
        sl = pl.ds(ci * c, c)
        zf = f_ref[sl, :]
        t = jnp.exp(-jnp.abs(zf))
        r = 1.0 / (1.0 + t)
        pos = zf >= 0.0
        sig = jnp.where(pos, r, t * r)
        nsig = jnp.where(pos, t * r, r)
        log_f = jnp.log(lb_all + (1.0 - lb_all) * sig)
        k = (1.0 - lb_all) * nsig
        if t_valid is not None:
            tpos = (tb * n_chunks + ci) * c + lax.broadcasted_iota(jnp.int32, (c, 1), 0)
            valid = tpos < t_valid
            log_f = jnp.where(valid, log_f, 0.0)
            k = jnp.where(valid, k, 0.0)
        cum = jnp.dot(tri3, jnp.concatenate(_split3(log_f), axis=0),
                      preferred_element_type=F32)
        ref = cum[c // 2:c // 2 + 1, :]
        last = cum[c - 1:c, :]
        q = _silu(q_ref[sl, :])
        qe = (q * jnp.exp(cum - ref)).astype(BF16)
        ke = (k * jnp.exp(ref - cum)).astype(BF16)
        qd = (q * jnp.exp(cum)).astype(BF16)
        kd = (k * jnp.exp(last - cum)).astype(BF16)
        decay = jnp.exp(last)
        vb_all = i_ref[sl, :].astype(BF16)
        normed = []
        for hh in range(n_heads):
            ks = slice(hh * hk, (hh + 1) * hk)
            vb = vb_all[:, hh * hv:(hh + 1) * hv]
            scores = lax.dot_general(qe[:, ks], ke[:, ks], NT_DIMS, preferred_element_type=F32)
            scores = jnp.where(causal, scores, 0.0).astype(BF16)
            st = st_ref[hh]
            o = (jnp.dot(scores, vb, preferred_element_type=F32)
                 + lax.dot_general(qd[:, ks], st.astype(BF16), NT_DIMS, preferred_element_type=F32))
            st_ref[hh] = decay[:, ks] * st + lax.dot_general(vb, kd[:, ks], TN_DIMS,
                                                            preferred_element_type=F32)
            ms = jnp.mean(o * o, axis=-1, keepdims=True)
            normed.append(o * lax.rsqrt(ms + RMS_EPS))
        on = jnp.concatenate(normed, axis=1)
        o_ref[sl, :] = (on * gain_all * _silu(g_ref[sl, :])).astype(o_ref.dtype)

    @pl.when(tb == pl.num_programs(2) - 1)
    def _():
        for hh in range(n_heads):
            s_ref[hh] = st_ref[hh].T


def _hgrn_call(u3, lb_logits, norm_gain, s0, slot, heads, hk, hv, t_valid=None):
    b, t, _ = u3.shape
    tblk = min(t, HGRN_TBLK)
    n_chunks = tblk // HGRN_CHUNK
    hb = HGRN_HEADS_PER_STEP
    groups = heads // hb
    qk = heads * hk
    v_off = 2 * qk // (hb * hv)
    n_lb = lb_logits.shape[0]
    kern = functools.partial(_hgrn_kernel, slot=slot, chunk=HGRN_CHUNK, n_chunks=n_chunks,
                             n_heads=hb, hk=hk, hv=hv, t_valid=t_valid)
    return pl.pallas_call(
        kern,
        out_shape=(jax.ShapeDtypeStruct((b, t, heads * hv), BF16),
                   jax.ShapeDtypeStruct((b, heads, hk, hv), F32)),
        grid=(b, groups, t // tblk),
        in_specs=[
            pl.BlockSpec((n_lb, hb * hk), lambda bi, g, tb: (0, g)),
            pl.BlockSpec((None, 1, hv), lambda bi, g, tb: (slot, 0, 0)),
            pl.BlockSpec((None, tblk, hb * hk), lambda bi, g, tb: (bi, tb, g)),
            pl.BlockSpec((None, tblk, hb * hk), lambda bi, g, tb: (bi, tb, groups + g)),
            pl.BlockSpec((None, tblk, hb * hv), lambda bi, g, tb: (bi, tb, v_off + g)),
            pl.BlockSpec((None, tblk, hb * hv), lambda bi, g, tb: (bi, tb, v_off + groups + g)),
            pl.BlockSpec((None, hb, hk, hv), lambda bi, g, tb: (bi, g, 0, 0)),
        ],
        out_specs=(
            pl.BlockSpec((None, tblk, hb * hv), lambda bi, g, tb: (bi, tb, g)),
            pl.BlockSpec((None, hb, hk, hv), lambda bi, g, tb: (bi, g, 0, 0)),
        ),
        scratch_shapes=[pltpu.VMEM((hb, hv, hk), F32)],
        compiler_params=pltpu.CompilerParams(
            dimension_semantics=("arbitrary", "arbitrary", "arbitrary"), vmem_limit_bytes=_vmem(40)),
        name="hgrn_mixer",
    )(lb_logits, norm_gain.reshape(-1, 1, hv), u3, u3, u3, u3, s0)


def _sbp_kernel(bias_ref, q_ref, k_ref, v_ref, g_ref, o_ref, ko_ref, vo_ref, acc_ref, *, tq, tk, scale):
    h = pl.program_id(1)
    qi = pl.program_id(2)

    @pl.when(qi == 0)
    def _():
        ko_ref[...] = k_ref[...].astype(ko_ref.dtype)
        vo_ref[...] = v_ref[...].astype(vo_ref.dtype)

    bias = bias_ref[h]
    q = (q_ref[...] * scale).astype(BF16)
    row = lax.broadcasted_iota(jnp.int32, (tk, tk), 0)
    col = lax.broadcasted_iota(jnp.int32, (tk, tk), 1)
    suffix = jnp.where(row >= col, 1.0, 0.0).astype(BF16)
    suffix2 = jnp.concatenate([suffix, suffix], axis=0)
    acc_ref[...] = jnp.zeros_like(acc_ref)

    def pair(start, carry, r0):
        masked = r0 is not None
        r0 = r0 or 0
        rows = tq - r0
        kb = k_ref[pl.ds(start, 2 * tk), :].astype(BF16)
        vb = v_ref[pl.ds(start, 2 * tk), :].astype(BF16)
        z = lax.dot_general(q[r0:], kb, NT_DIMS, preferred_element_type=F32) + bias
        sp = _softplus(z)
        if masked:
            q_pos = qi * tq + r0 + lax.broadcasted_iota(jnp.int32, (rows, 2 * tk), 0)
            k_pos = start + lax.broadcasted_iota(jnp.int32, (rows, 2 * tk), 1)
            keep = k_pos < q_pos
            sp = jnp.where(keep, sp, 0.0)
        hi, lo = _split2(sp)
        rs_new = jnp.dot(jnp.concatenate([hi[:, tk:], lo[:, tk:]], axis=1), suffix2,
                         preferred_element_type=F32)
        rs_old = jnp.dot(jnp.concatenate([hi[:, :tk], lo[:, :tk]], axis=1), suffix2,
                         preferred_element_type=F32)
        carry_old = carry + rs_new[:, 0:1]
        w = jnp.concatenate([jnp.exp(z[:, :tk] - rs_old - carry_old),
                             jnp.exp(z[:, tk:] - rs_new - carry)], axis=1)
        if masked:
            w = jnp.where(keep, w, 0.0)
        acc_ref[r0:, :] += jnp.dot(w.astype(BF16), vb, preferred_element_type=F32)
        return carry_old + rs_old[:, 0:1]

    span = 2 * tk
    n_diag = tq // span
    carry = jnp.zeros((tq, 1), F32)
    for dj in range(n_diag):
        r0 = (n_diag - 1 - dj) * span
        c_new = pair(pl.multiple_of(qi * tq + r0, span), carry[r0:], r0)
        carry = c_new if r0 == 0 else jnp.concatenate([carry[:r0], c_new], axis=0)

    def body(jj, c):
        for dj in range(n_diag):
            c = pair(pl.multiple_of(qi * tq - (jj * n_diag + dj + 1) * span, span), c, None)
        return c

    lax.fori_loop(0, qi, body, carry)
    o_ref[...] = (acc_ref[...] * _silu(g_ref[...])).astype(o_ref.dtype)


def _sbp_call(u3, bias, heads, hk, hv, row_dtypes):
    b, t, _ = u3.shape
    tq = min(t, SBP_TQ)
    tk = min(tq // 2, SBP_TK)
    qk = heads * hk
    kv = qk // hv
    kern = functools.partial(_sbp_kernel, tq=tq, tk=tk, scale=float(hk) ** -0.5)
    grid_spec = pltpu.PrefetchScalarGridSpec(
        num_scalar_prefetch=1,
        grid=(b, heads, t // tq),
        in_specs=[
            pl.BlockSpec((None, tq, hk), lambda bi, h, qi, bias: (bi, qi, h)),
            pl.BlockSpec((None, t, hk), lambda bi, h, qi, bias: (bi, 0, heads + h)),
            pl.BlockSpec((None, t, hv), lambda bi, h, qi, bias: (bi, 0, 2 * kv + h)),
            pl.BlockSpec((None, tq, hv), lambda bi, h, qi, bias: (bi, qi, 2 * kv + heads + h)),
        ],
        out_specs=(
            pl.BlockSpec((None, tq, hv), lambda bi, h, qi, bias: (bi, qi, h)),
            pl.BlockSpec((None, t, hk), lambda bi, h, qi, bias: (bi, 0, h)),
            pl.BlockSpec((None, t, hv), lambda bi, h, qi, bias: (bi, 0, h)),
        ),
        scratch_shapes=[pltpu.VMEM((tq, hv), F32)],
    )
    return pl.pallas_call(
        kern,
        out_shape=(jax.ShapeDtypeStruct((b, t, heads * hv), BF16),
                   jax.ShapeDtypeStruct((b, t, qk), row_dtypes[0]),
                   jax.ShapeDtypeStruct((b, t, heads * hv), row_dtypes[1])),
        grid_spec=grid_spec,
        compiler_params=pltpu.CompilerParams(
            dimension_semantics=("arbitrary", "arbitrary", "arbitrary"), vmem_limit_bytes=_vmem(56)),
        name="sb_prompt",
    )(bias, u3, u3, u3, u3)


def _sbs_kernel(pt_ref, bias_ref, wq_ref, kn_ref, vn_ref, *rest, pages_per_step, t_new, heads, hv,
                row_stride, row_start, scale):
    p = pages_per_step
    k_refs = rest[:p]
    v_refs = rest[p:2 * p]
    g_ref, o_ref, acc_ref, carry_ref, wbuf_ref = rest[2 * p:]
    s = pl.program_id(1)
    n_keys = kn_ref.shape[0]
    n_cols = wq_ref.shape[1]

    row = lax.broadcasted_iota(jnp.int32, (n_keys, n_keys), 0)
    col = lax.broadcasted_iota(jnp.int32, (n_keys, n_keys), 1)
    newer = jnp.where(col > row, 1.0, 0.0).astype(BF16)
    newer2 = jnp.concatenate([newer, newer], axis=1)
    hsel_h = lax.broadcasted_iota(jnp.int32, (row_stride, n_cols), 0) - row_start
    hsel_c = lax.broadcasted_iota(jnp.int32, (row_stride, n_cols), 1)
    head_sel = jnp.where(jnp.logical_and(hsel_c // t_new == hsel_h, hsel_c < heads * t_new), 1.0, 0.0)

    def weights(kflat, mask):
        zt = jnp.dot(kflat, wq_ref[...], preferred_element_type=F32) * scale + bias_ref[...]
        sp = _softplus(zt)
        if mask is not None:
            sp = jnp.where(mask, sp, 0.0)
        hi, lo = _split2(sp)
        later = jnp.dot(newer2, jnp.concatenate([hi, lo], axis=0), preferred_element_type=F32)
        w = jnp.exp(zt - sp - later - carry_ref[...])
        if mask is not None:
            w = jnp.where(mask, w, 0.0)
        carry_ref[...] += later[0:1, :] + sp[0:1, :]
        return w

    def head_rows(ref, h):
        return ref[pl.ds(row_start + h, n_keys, stride=row_stride), :]

    @pl.when(s == 0)
    def _():
        acc_ref[...] = jnp.zeros_like(acc_ref)
        carry_ref[...] = jnp.zeros_like(carry_ref)
        key_pos = lax.broadcasted_iota(jnp.int32, (n_keys, n_cols), 0)
        q_pos = lax.broadcasted_iota(jnp.int32, (n_keys, n_cols), 1) % t_new
        w = weights(kn_ref[...].astype(BF16), key_pos < q_pos)
        for h in range(heads):
            wh = (w * head_sel[row_start + h:row_start + h + 1, :]).astype(BF16)
            acc_ref[...] += lax.dot_general(wh, vn_ref[:, h * hv:(h + 1) * hv].astype(BF16), TN_DIMS,
                                            preferred_element_type=F32)

    @pl.when(s > 0)
    def _():
        for i in range(p):
            kflat = jnp.concatenate([head_rows(k_refs[i], h).astype(BF16) for h in range(heads)], axis=1)
            wbuf_ref[...] = weights(kflat, None)
            wexp = jnp.concatenate(
                [jnp.broadcast_to(wbuf_ref[key:key + 1, :], (row_stride, n_cols)) * head_sel
                 for key in range(n_keys)], axis=0).astype(BF16)
            acc_ref[...] += lax.dot_general(wexp, v_refs[i][...].astype(BF16), TN_DIMS,
                                            preferred_element_type=F32)

    @pl.when(s == pl.num_programs(1) - 1)
    def _():
        for h in range(heads):
            o = acc_ref[h * t_new:(h + 1) * t_new, :]
            o_ref[:, h * hv:(h + 1) * hv] = (o * _silu(g_ref[:, h * hv:(h + 1) * hv])).astype(o_ref.dtype)


def _sbs_call(us3, cache_k, cache_v, page_table, bias, slot, heads, hk, hv):
    db, t_new, _ = us3.shape
    n_pool, page, n_b = cache_k.shape[:3]
    n_pages = page_table.shape[1]
    qk = heads * hk
    br = heads * hv
    n_cols = -(-heads * t_new // 128) * 128
    p = 4
    while n_pages % p:
        p //= 2
    q = us3[..., :qk].reshape(db, t_new, heads, hk)
    eye = jnp.eye(heads, dtype=F32)
    wq = (q.transpose(0, 2, 3, 1)[:, :, :, None, :] * eye[None, :, None, :, None])
    wq = wq.reshape(db, qk, heads * t_new)
    wq = jnp.pad(wq, ((0, 0), (0, 0), (0, n_cols - heads * t_new))).astype(BF16)
    pad = ((0, 0), (0, page - t_new), (0, 0))
    k_new = jnp.pad(us3[..., qk:2 * qk], pad)
    v_new = jnp.pad(us3[..., 2 * qk:2 * qk + br], pad)
    g3 = us3[..., 2 * qk + br:]
    bias_row = jnp.pad(jnp.repeat(bias.astype(F32), t_new), (0, n_cols - heads * t_new)).reshape(1, n_cols)
    rows = page * n_b * heads
    ck = cache_k.reshape(n_pool, rows, hk)
    cv = cache_v.reshape(n_pool, rows, hv)

    def page_map(i):
        def index_map(b, s, pt):
            logical = n_pages - 1 - (jnp.maximum(s - 1, 0) * p + i)
            return (pt[b, logical], 0, 0)
        return index_map

    kern = functools.partial(_sbs_kernel, pages_per_step=p, t_new=t_new, heads=heads, hv=hv,
                             row_stride=n_b * heads, row_start=slot * heads, scale=float(hk) ** -0.5)
    grid_spec = pltpu.PrefetchScalarGridSpec(
        num_scalar_prefetch=1,
        grid=(db, 1 + n_pages // p),
        in_specs=(
            [pl.BlockSpec((1, n_cols), lambda b, s, pt: (0, 0)),
             pl.BlockSpec((None, qk, n_cols), lambda b, s, pt: (b, 0, 0)),
             pl.BlockSpec((None, page, qk), lambda b, s, pt: (b, 0, 0)),
             pl.BlockSpec((None, page, br), lambda b, s, pt: (b, 0, 0))]
            + [pl.BlockSpec((None, rows, hk), page_map(i)) for i in range(p)]
            + [pl.BlockSpec((None, rows, hv), page_map(i)) for i in range(p)]
            + [pl.BlockSpec((None, t_new, br), lambda b, s, pt: (b, 0, 0))]),
        out_specs=pl.BlockSpec((None, t_new, br), lambda b, s, pt: (b, 0, 0)),
        scratch_shapes=[pltpu.VMEM((n_cols, hv), F32), pltpu.VMEM((1, n_cols), F32),
                        pltpu.VMEM((page, n_cols), F32)],
    )
    return pl.pallas_call(
        kern,
        out_shape=jax.ShapeDtypeStruct((db, t_new, br), BF16),
        grid_spec=grid_spec,
        compiler_params=pltpu.CompilerParams(
            dimension_semantics=("arbitrary", "arbitrary"), vmem_limit_bytes=_vmem(48)),
        name="sb_decode",
    )(page_table, bias_row, wq, k_new, v_new, *([ck] * p), *([cv] * p), g3)


def kernel(x_prompt, x_sample, state_hgrn, cache_k, cache_v, page_table, c_prompt, c_sample, w_ada, b_ada,
           w_in, w_out, hgrn_lb_logits, hgrn_norm_gain, sb_logit_bias, ln_gain, ln_bias):
    b, t, d = x_prompt.shape
    db, dt, _ = x_sample.shape
    depth = w_in.shape[0]
    _, _, heads, hk, hv = state_hgrn.shape
    qk = heads * hk
    br = heads * hv
    in_width = w_in.shape[-1]
    alpha = (2.0 * depth) ** 0.25

    n_c = b + db
    c_all = jnp.pad(jnp.concatenate([c_prompt, c_sample], axis=0), ((0, (-n_c) % 8), (0, 0)))
    mods = _ada_call(c_all, w_ada, b_ada)
    w_in_bf = w_in.astype(BF16)
    w_out_bf = w_out.astype(BF16)

    tm_p = min(t, 1024)
    tm_o = min(t, 512)
    xp = x_prompt.reshape(b * t, d)
    xs = x_sample.reshape(db * dt, d)
    t_pad = -(-dt // HGRN_CHUNK) * HGRN_CHUNK
    hgrn_p, hgrn_s, kp_rows, vp_rows, ks_rows, vs_rows = [], [], [], [], [], []
    for layer in range(depth):
        slot = layer // N_MIXERS
        m = mods[layer]
        mp = [m[:b, i * d:(i + 1) * d].reshape(b, 1, d) for i in range(3)]
        ms = [jnp.repeat(m[b:n_c, i * d:(i + 1) * d], dt, axis=0).reshape(1, db * dt, d) for i in range(3)]
        up, us_copies = _inproj_call(xp, mp[0], mp[1], xs, ms[0][0], ms[1][0], w_in_bf, layer, t, tm_p)
        us = us_copies[0]
        up3 = up.reshape(b, t, in_width)
        us3 = us.reshape(db, dt, in_width)
        if layer % N_MIXERS == 0:
            bp, sp = _hgrn_call(up3, hgrn_lb_logits, hgrn_norm_gain, jnp.zeros((b, heads, hk, hv), F32),
                                slot, heads, hk, hv)
            us_pad = jnp.pad(us3, ((0, 0), (0, t_pad - dt), (0, 0)))
            bs, ss = _hgrn_call(us_pad, hgrn_lb_logits, hgrn_norm_gain, state_hgrn[slot],
                                slot, heads, hk, hv, t_valid=dt)
            bs = bs[:, :dt]
            hgrn_p.append(sp.astype(state_hgrn.dtype))
            hgrn_s.append(ss.astype(state_hgrn.dtype))
        else:
            bias = sb_logit_bias[slot].astype(F32)
            bp, kp_flat, vp_flat = _sbp_call(up3, bias, heads, hk, hv, (cache_k.dtype, cache_v.dtype))
            bs = _sbs_call(us3, cache_k, cache_v, page_table, bias, slot, heads, hk, hv)
            kp_rows.append(kp_flat.reshape(b, t, heads, hk))
            vp_rows.append(vp_flat.reshape(b, t, heads, hv))
            ks_rows.append(us3[..., qk:2 * qk].reshape(db, dt, heads, hk).astype(cache_k.dtype))
            vs_rows.append(us3[..., 2 * qk:2 * qk + br].reshape(db, dt, heads, hv).astype(cache_v.dtype))
        xp = _outproj_call(xp, bp.reshape(b * t, br), w_out_bf, mp[2], ln_gain, ln_bias, layer, t, tm_o, alpha)
        xs = _outproj_call(xs, bs.reshape(db * dt, br), w_out_bf, ms[2], ln_gain, ln_bias, layer,
                           db * dt, db * dt, alpha)
    return (xp.reshape(b, t, d), xs.reshape(db, dt, d),
            jnp.stack(hgrn_p, axis=0), jnp.stack(hgrn_s, axis=0),
            jnp.stack(kp_rows, axis=2), jnp.stack(vp_rows, axis=2),
            jnp.stack(ks_rows, axis=2), jnp.stack(vs_rows, axis=2))
```

```python
import functools

import jax
import jax.numpy as jnp
from jax import lax
from jax.experimental import pallas as pl
from jax.experimental.pallas import tpu as pltpu

F32 = jnp.float32
BF16 = jnp.bfloat16

LN_EPS = 1e-5
RMS_EPS = 1e-6
HGRN_CHUNK = 64
HGRN_TBLK = 128
HGRN_HEADS_PER_STEP = 16
SBP_TQ = 1024
SBP_TK = 256
N_MIXERS = 2

NT_DIMS = (((1,), (1,)), ((), ()))
TN_DIMS = (((0,), (0,)), ((), ()))


def _vmem(mib):
    return mib * 1024 * 1024


def _sigmoid(x):
    return 1.0 / (1.0 + jnp.exp(-x))


def _silu(x):
    return x * _sigmoid(x)


def _softplus(z):
    return jnp.maximum(z, 0.0) + jnp.log(1.0 + jnp.exp(-jnp.abs(z)))


def _split2(x):
    hi = x.astype(BF16)
    lo = (x - hi.astype(F32)).astype(BF16)
    return hi, lo


def _split3(x):
    hi = x.astype(BF16)
    r = x - hi.astype(F32)
    mid = r.astype(BF16)
    lo = (r - mid.astype(F32)).astype(BF16)
    return hi, mid, lo


def _ada_kernel(c_ref, w_ref, b_ref, o_ref):
    o_ref[...] = jnp.dot(c_ref[...], w_ref[...], precision=lax.Precision.HIGHEST,
                         preferred_element_type=F32) + b_ref[...]


def _ada_call(c_all, w_ada, b_ada):
    depth, d, n = w_ada.shape
    rows = c_all.shape[0]
    tn = 2048
    return pl.pallas_call(
        _ada_kernel,
        out_shape=jax.ShapeDtypeStruct((depth, rows, n), F32),
        grid=(depth, n // tn),
        in_specs=[
            pl.BlockSpec((rows, d), lambda l, j: (0, 0)),
            pl.BlockSpec((None, d, tn), lambda l, j: (l, 0, j)),
            pl.BlockSpec((None, 1, tn), lambda l, j: (l, 0, j)),
        ],
        out_specs=pl.BlockSpec((None, rows, tn), lambda l, j: (l, 0, j)),
        compiler_params=pltpu.CompilerParams(
            dimension_semantics=("arbitrary", "arbitrary"), vmem_limit_bytes=_vmem(40)),
        name="ada_mod",
    )(c_all, w_ada, b_ada.reshape(depth, 1, n))


def _inproj_kernel(x_ref, shift_ref, scale_ref, xs_ref, shift_s_ref, scale_s_ref, w_ref, u_ref, us_ref, h_ref,
                   *, tm):
    @pl.when(pl.program_id(1) == 0)
    def _():
        h_ref[:tm, :] = (x_ref[...] * (1.0 + scale_ref[...]) + shift_ref[...]).astype(BF16)
        h_ref[tm:, :] = (xs_ref[...] * (1.0 + scale_s_ref[...]) + shift_s_ref[...]).astype(BF16)

    u = jnp.dot(h_ref[...], w_ref[...], preferred_element_type=F32)
    u_ref[...] = u[:tm]
    us_ref[...] = u[tm:]


def _inproj_call(x2, shift, scale, xs2, shift_s, scale_s, w_bf, layer, rows_per_mod, tm):
    m, d = x2.shape
    ms = xs2.shape[0]
    n = w_bf.shape[-1]
    tn = 1024
    tiles_per_mod = rows_per_mod // tm
    mod_spec = pl.BlockSpec((None, 1, d), lambda i, j: (i // tiles_per_mod, 0, 0))
    s_spec = pl.BlockSpec((ms, d), lambda i, j: (0, 0))
    return pl.pallas_call(
        functools.partial(_inproj_kernel, tm=tm),
        out_shape=(jax.ShapeDtypeStruct((m, n), F32), jax.ShapeDtypeStruct((m // tm, ms, n), F32)),
        grid=(m // tm, n // tn),
        in_specs=[
            pl.BlockSpec((tm, d), lambda i, j: (i, 0)),
            mod_spec, mod_spec,
            s_spec, s_spec, s_spec,
            pl.BlockSpec((None, d, tn), lambda i, j: (layer, 0, j)),
        ],
        out_specs=(pl.BlockSpec((tm, tn), lambda i, j: (i, j)),
                   pl.BlockSpec((None, ms, tn), lambda i, j: (i, 0, j))),
        scratch_shapes=[pltpu.VMEM((tm + ms, d), BF16)],
        compiler_params=pltpu.CompilerParams(
            dimension_semantics=("arbitrary", "arbitrary"), vmem_limit_bytes=_vmem(48)),
        name="in_proj",
    )(x2, shift, scale, xs2, shift_s, scale_s, w_bf)


def _outproj_kernel(x_ref, br_ref, w_ref, gate_ref, lng_ref, lnb_ref, y_ref, *, alpha):
    v = alpha * x_ref[...] + gate_ref[...] * jnp.dot(br_ref[...], w_ref[...], preferred_element_type=F32)
    mu = jnp.mean(v, axis=-1, keepdims=True)
    dv = v - mu
    var = jnp.mean(dv * dv, axis=-1, keepdims=True)
    y_ref[...] = dv * lax.rsqrt(var + LN_EPS) * lng_ref[...] + lnb_ref[...]


def _outproj_call(x2, br2, w_bf, gate, ln_gain, ln_bias, layer, rows_per_mod, tm, alpha):
    m, d = x2.shape
    kdim = br2.shape[1]
    r = gate.shape[1]
    tiles_per_mod = rows_per_mod // tm
    depth = ln_gain.shape[0]
    vec_spec = pl.BlockSpec((None, 1, d), lambda i: (layer, 0, 0))
    return pl.pallas_call(
        functools.partial(_outproj_kernel, alpha=alpha),
        out_shape=jax.ShapeDtypeStruct((m, d), F32),
        grid=(m // tm,),
        in_specs=[
            pl.BlockSpec((tm, d), lambda i: (i, 0)),
            pl.BlockSpec((tm, kdim), lambda i: (i, 0)),
            pl.BlockSpec((None, kdim, d), lambda i: (layer, 0, 0), pipeline_mode=pl.Buffered(1)),
            pl.BlockSpec((None, r, d), lambda i: (i // tiles_per_mod, 0, 0)),
            vec_spec, vec_spec,
        ],
        out_specs=pl.BlockSpec((tm, d), lambda i: (i, 0)),
        compiler_params=pltpu.CompilerParams(
            dimension_semantics=("arbitrary",), vmem_limit_bytes=_vmem(52)),
        name="out_proj_ln",
    )(x2, br2, w_bf, gate, ln_gain.reshape(depth, 1, d), ln_bias.reshape(depth, 1, d))


def _hgrn_kernel(lbl_ref, gain_ref, q_ref, f_ref, i_ref, g_ref, s0_ref, o_ref, s_ref, st_ref, *,
                 slot, chunk, n_chunks, n_heads, hk, hv, t_valid):
    tb = pl.program_id(2)
    c = chunk

    @pl.when(tb == 0)
    def _():
        for hh in range(n_heads):
            st_ref[hh] = s0_ref[hh].T

    lg = lbl_ref[...]
    e = jnp.exp(lg - jnp.max(lg, axis=0, keepdims=True))
    lb_all = jnp.sum(e[:slot + 1], axis=0, keepdims=True) / jnp.sum(e, axis=0, keepdims=True)
    gain = gain_ref[...]

    row = lax.broadcasted_iota(jnp.int32, (c, c), 0)
    col = lax.broadcasted_iota(jnp.int32, (c, c), 1)
    causal = col <= row
    tri = jnp.where(causal, 1.0, 0.0).astype(BF16)
    tri3 = jnp.concatenate([tri, tri, tri], axis=1)

    gain_all = jnp.concatenate([gain] * n_heads, axis=1)
    for ci in range(n_chunks):
        sl = pl.ds(ci * c, c)
        zf = f_ref[sl, :]
        t = jnp.exp(-jnp.abs(zf))
        r = 1.0 / (1.0 + t)
        pos = zf >= 0.0
        sig = jnp.where(pos, r, t * r)
        nsig = jnp.where(pos, t * r, r)
        log_f = jnp.log(lb_all + (1.0 - lb_all) * sig)
        k = (1.0 - lb_all) * nsig
        if t_valid is not None:
            tpos = (tb * n_chunks + ci) * c + lax.broadcasted_iota(jnp.int32, (c, 1), 0)
            valid = tpos < t_valid
            log_f = jnp.where(valid, log_f, 0.0)
            k = jnp.where(valid, k, 0.0)
        cum = jnp.dot(tri3, jnp.concatenate(_split3(log_f), axis=0),
                      preferred_element_type=F32)
        ref = cum[c // 2:c // 2 + 1, :]
        last = cum[c - 1:c, :]
        q = _silu(q_ref[sl, :])
        qe = (q * jnp.exp(cum - ref)).astype(BF16)
        ke = (k * jnp.exp(ref - cum)).astype(BF16)
        qd = (q * jnp.exp(cum)).astype(BF16)
        kd = (k * jnp.exp(last - cum)).astype(BF16)
        decay = jnp.exp(last)
        vb_all = i_ref[sl, :].astype(BF16)
        normed = []
        for hh in range(n_heads):
            ks = slice(hh * hk, (hh + 1) * hk)
            vb = vb_all[:, hh * hv:(hh + 1) * hv]
            scores = lax.dot_general(qe[:, ks], ke[:, ks], NT_DIMS, preferred_element_type=F32)
            scores = jnp.where(causal, scores, 0.0).astype(BF16)
            st = st_ref[hh]
            o = (jnp.dot(scores, vb, preferred_element_type=F32)
                 + lax.dot_general(qd[:, ks], st.astype(BF16), NT_DIMS, preferred_element_type=F32))
            st_ref[hh] = decay[:, ks] * st + lax.dot_general(vb, kd[:, ks], TN_DIMS,
                                                            preferred_element_type=F32)
            ms = jnp.mean(o * o, axis=-1, keepdims=True)
            normed.append(o * lax.rsqrt(ms + RMS_EPS))
        on = jnp.concatenate(normed, axis=1)
        o_ref[sl, :] = (on * gain_all * _silu(g_ref[sl, :])).astype(o_ref.dtype)

    @pl.when(tb == pl.num_programs(2) - 1)
    def _():
        for hh in range(n_heads):
            s_ref[hh] = st_ref[hh].T


def _hgrn_call(u3, lb_logits, norm_gain, s0, slot, heads, hk, hv, t_valid=None):
    b, t, _ = u3.shape
    tblk = min(t, HGRN_TBLK)
    n_chunks = tblk // HGRN_CHUNK
    hb = HGRN_HEADS_PER_STEP
    groups = heads // hb
    qk = heads * hk
    v_off = 2 * qk // (hb * hv)
    n_lb = lb_logits.shape[0]
    kern = functools.partial(_hgrn_kernel, slot=slot, chunk=HGRN_CHUNK, n_chunks=n_chunks,
                             n_heads=hb, hk=hk, hv=hv, t_valid=t_valid)
    return pl.pallas_call(
        kern,
        out_shape=(jax.ShapeDtypeStruct((b, t, heads * hv), BF16),
                   jax.ShapeDtypeStruct((b, heads, hk, hv), F32)),
        grid=(b, groups, t // tblk),
        in_specs=[
            pl.BlockSpec((n_lb, hb * hk), lambda bi, g, tb: (0, g)),
            pl.BlockSpec((None, 1, hv), lambda bi, g, tb: (slot, 0, 0)),
            pl.BlockSpec((None, tblk, hb * hk), lambda bi, g, tb: (bi, tb, g)),
            pl.BlockSpec((None, tblk, hb * hk), lambda bi, g, tb: (bi, tb, groups + g)),
            pl.BlockSpec((None, tblk, hb * hv), lambda bi, g, tb: (bi, tb, v_off + g)),
            pl.BlockSpec((None, tblk, hb * hv), lambda bi, g, tb: (bi, tb, v_off + groups + g)),
            pl.BlockSpec((None, hb, hk, hv), lambda bi, g, tb: (bi, g, 0, 0)),
        ],
        out_specs=(
            pl.BlockSpec((None, tblk, hb * hv), lambda bi, g, tb: (bi, tb, g)),
            pl.BlockSpec((None, hb, hk, hv), lambda bi, g, tb: (bi, g, 0, 0)),
        ),
        scratch_shapes=[pltpu.VMEM((hb, hv, hk), F32)],
        compiler_params=pltpu.CompilerParams(
            dimension_semantics=("arbitrary", "arbitrary", "arbitrary"), vmem_limit_bytes=_vmem(40)),
        name="hgrn_mixer",
    )(lb_logits, norm_gain.reshape(-1, 1, hv), u3, u3, u3, u3, s0)


def _sbp_kernel(bias_ref, q_ref, k_ref, v_ref, g_ref, o_ref, ko_ref, vo_ref, acc_ref, *, tq, tk, scale):
    h = pl.program_id(1)
    qi = pl.program_id(2)

    @pl.when(qi == 0)
    def _():
        ko_ref[...] = k_ref[...].astype(ko_ref.dtype)
        vo_ref[...] = v_ref[...].astype(vo_ref.dtype)

    bias = bias_ref[h]
    q = (q_ref[...] * scale).astype(BF16)
    row = lax.broadcasted_iota(jnp.int32, (tk, tk), 0)
    col = lax.broadcasted_iota(jnp.int32, (tk, tk), 1)
    suffix = jnp.where(row >= col, 1.0, 0.0).astype(BF16)
    suffix2 = jnp.concatenate([suffix, suffix], axis=0)
    acc_ref[...] = jnp.zeros_like(acc_ref)

    def pair(start, carry, r0):
        masked = r0 is not None
        r0 = r0 or 0
        rows = tq - r0
        kb = k_ref[pl.ds(start, 2 * tk), :].astype(BF16)
        vb = v_ref[pl.ds(start, 2 * tk), :].astype(BF16)
        z = lax.dot_general(q[r0:], kb, NT_DIMS, preferred_element_type=F32) + bias
        sp = _softplus(z)
        if masked:
            q_pos = qi * tq + r0 + lax.broadcasted_iota(jnp.int32, (rows, 2 * tk), 0)
            k_pos = start + lax.broadcasted_iota(jnp.int32, (rows, 2 * tk), 1)
            keep = k_pos < q_pos
            sp = jnp.where(keep, sp, 0.0)
        hi, lo = _split2(sp)
        rs_new = jnp.dot(jnp.concatenate([hi[:, tk:], lo[:, tk:]], axis=1), suffix2,
                         preferred_element_type=F32)
        rs_old = jnp.dot(jnp.concatenate([hi[:, :tk], lo[:, :tk]], axis=1), suffix2,
                         preferred_element_type=F32)
        carry_old = carry + rs_new[:, 0:1]
        w = jnp.concatenate([jnp.exp(z[:, :tk] - rs_old - carry_old),
                             jnp.exp(z[:, tk:] - rs_new - carry)], axis=1)
        if masked:
            w = jnp.where(keep, w, 0.0)
        acc_ref[r0:, :] += jnp.dot(w.astype(BF16), vb, preferred_element_type=F32)
        return carry_old + rs_old[:, 0:1]

    span = 2 * tk
    n_diag = tq // span
    carry = jnp.zeros((tq, 1), F32)
    for dj in range(n_diag):
        r0 = (n_diag - 1 - dj) * span
        c_new = pair(pl.multiple_of(qi * tq + r0, span), carry[r0:], r0)
        carry = c_new if r0 == 0 else jnp.concatenate([carry[:r0], c_new], axis=0)

    def body(jj, c):
        for dj in range(n_diag):
            c = pair(pl.multiple_of(qi * tq - (jj * n_diag + dj + 1) * span, span), c, None)
        return c

    lax.fori_loop(0, qi, body, carry)
    o_ref[...] = (acc_ref[...] * _silu(g_ref[...])).astype(o_ref.dtype)


def _sbp_call(u3, bias, heads, hk, hv, row_dtypes):
    b, t, _ = u3.shape
    tq = min(t, SBP_TQ)
    tk = min(tq // 2, SBP_TK)
    qk = heads * hk
    kv = qk // hv
    kern = functools.partial(_sbp_kernel, tq=tq, tk=tk, scale=float(hk) ** -0.5)
    grid_spec = pltpu.PrefetchScalarGridSpec(
        num_scalar_prefetch=1,
        grid=(b, heads, t // tq),
        in_specs=[
            pl.BlockSpec((None, tq, hk), lambda bi, h, qi, bias: (bi, qi, h)),
            pl.BlockSpec((None, t, hk), lambda bi, h, qi, bias: (bi, 0, heads + h)),
            pl.BlockSpec((None, t, hv), lambda bi, h, qi, bias: (bi, 0, 2 * kv + h)),
            pl.BlockSpec((None, tq, hv), lambda bi, h, qi, bias: (bi, qi, 2 * kv + heads + h)),
        ],
        out_specs=(
            pl.BlockSpec((None, tq, hv), lambda bi, h, qi, bias: (bi, qi, h)),
            pl.BlockSpec((None, t, hk), lambda bi, h, qi, bias: (bi, 0, h)),
            pl.BlockSpec((None, t, hv), lambda bi, h, qi, bias: (bi, 0, h)),
        ),
        scratch_shapes=[pltpu.VMEM((tq, hv), F32)],
    )
    return pl.pallas_call(
        kern,
        out_shape=(jax.ShapeDtypeStruct((b, t, heads * hv), BF16),
                   jax.ShapeDtypeStruct((b, t, qk), row_dtypes[0]),
                   jax.ShapeDtypeStruct((b, t, heads * hv), row_dtypes[1])),
        grid_spec=grid_spec,
        compiler_params=pltpu.CompilerParams(
            dimension_semantics=("arbitrary", "arbitrary", "arbitrary"), vmem_limit_bytes=_vmem(56)),
        name="sb_prompt",
    )(bias, u3, u3, u3, u3)


def _sbs_kernel(pt_ref, bias_ref, wq_ref, kn_ref, vn_ref, *rest, pages_per_step, t_new, heads, hv,
                row_stride, row_start, scale):
    p = pages_per_step
    k_refs = rest[:p]
    v_refs = rest[p:2 * p]
    g_ref, o_ref, acc_ref, carry_ref, wbuf_ref = rest[2 * p:]
    s = pl.program_id(1)
    n_keys = kn_ref.shape[0]
    n_cols = wq_ref.shape[1]

    row = lax.broadcasted_iota(jnp.int32, (n_keys, n_keys), 0)
    col = lax.broadcasted_iota(jnp.int32, (n_keys, n_keys), 1)
    newer = jnp.where(col > row, 1.0, 0.0).astype(BF16)
    newer2 = jnp.concatenate([newer, newer], axis=1)
    hsel_h = lax.broadcasted_iota(jnp.int32, (row_stride, n_cols), 0) - row_start
    hsel_c = lax.broadcasted_iota(jnp.int32, (row_stride, n_cols), 1)
    head_sel = jnp.where(jnp.logical_and(hsel_c // t_new == hsel_h, hsel_c < heads * t_new), 1.0, 0.0)

    def weights(kflat, mask):
        zt = jnp.dot(kflat, wq_ref[...], preferred_element_type=F32) * scale + bias_ref[...]
        sp = _softplus(zt)
        if mask is not None:
            sp = jnp.where(mask, sp, 0.0)
        hi, lo = _split2(sp)
        later = jnp.dot(newer2, jnp.concatenate([hi, lo], axis=0), preferred_element_type=F32)
        w = jnp.exp(zt - sp - later - carry_ref[...])
        if mask is not None:
            w = jnp.where(mask, w, 0.0)
        carry_ref[...] += later[0:1, :] + sp[0:1, :]
        return w

    def head_rows(ref, h):
        return ref[pl.ds(row_start + h, n_keys, stride=row_stride), :]

    @pl.when(s == 0)
    def _():
        acc_ref[...] = jnp.zeros_like(acc_ref)
        carry_ref[...] = jnp.zeros_like(carry_ref)
        key_pos = lax.broadcasted_iota(jnp.int32, (n_keys, n_cols), 0)
        q_pos = lax.broadcasted_iota(jnp.int32, (n_keys, n_cols), 1) % t_new
        w = weights(kn_ref[...].astype(BF16), key_pos < q_pos)
        for h in range(heads):
            wh = (w * head_sel[row_start + h:row_start + h + 1, :]).astype(BF16)
            acc_ref[...] += lax.dot_general(wh, vn_ref[:, h * hv:(h + 1) * hv].astype(BF16), TN_DIMS,
                                            preferred_element_type=F32)

    @pl.when(s > 0)
    def _():
        for i in range(p):
            kflat = jnp.concatenate([head_rows(k_refs[i], h).astype(BF16) for h in range(heads)], axis=1)
            wbuf_ref[...] = weights(kflat, None)
            wexp = jnp.concatenate(
                [jnp.broadcast_to(wbuf_ref[key:key + 1, :], (row_stride, n_cols)) * head_sel
                 for key in range(n_keys)], axis=0).astype(BF16)
            acc_ref[...] += lax.dot_general(wexp, v_refs[i][...].astype(BF16), TN_DIMS,
                                            preferred_element_type=F32)

    @pl.when(s == pl.num_programs(1) - 1)
    def _():
        for h in range(heads):
            o = acc_ref[h * t_new:(h + 1) * t_new, :]
            o_ref[:, h * hv:(h + 1) * hv] = (o * _silu(g_ref[:, h * hv:(h + 1) * hv])).astype(o_ref.dtype)


def _sbs_call(us3, cache_k, cache_v, page_table, bias, slot, heads, hk, hv):
    db, t_new, _ = us3.shape
    n_pool, page, n_b = cache_k.shape[:3]
    n_pages = page_table.shape[1]
    qk = heads * hk
    br = heads * hv
    n_cols = -(-heads * t_new // 128) * 128
    p = 4
    while n_pages % p:
        p //= 2
    q = us3[..., :qk].reshape(db, t_new, heads, hk)
    eye = jnp.eye(heads, dtype=F32)
    wq = (q.transpose(0, 2, 3, 1)[:, :, :, None, :] * eye[None, :, None, :, None])
    wq = wq.reshape(db, qk, heads * t_new)
    wq = jnp.pad(wq, ((0, 0), (0, 0), (0, n_cols - heads * t_new))).astype(BF16)
    pad = ((0, 0), (0, page - t_new), (0, 0))
    k_new = jnp.pad(us3[..., qk:2 * qk], pad)
    v_new = jnp.pad(us3[..., 2 * qk:2 * qk + br], pad)
    g3 = us3[..., 2 * qk + br:]
    bias_row = jnp.pad(jnp.repeat(bias.astype(F32), t_new), (0, n_cols - heads * t_new)).reshape(1, n_cols)
    rows = page * n_b * heads
    ck = cache_k.reshape(n_pool, rows, hk)
    cv = cache_v.reshape(n_pool, rows, hv)

    def page_map(i):
        def index_map(b, s, pt):
            logical = n_pages - 1 - (jnp.maximum(s - 1, 0) * p + i)
            return (pt[b, logical], 0, 0)
        return index_map

    kern = functools.partial(_sbs_kernel, pages_per_step=p, t_new=t_new, heads=heads, hv=hv,
                             row_stride=n_b * heads, row_start=slot * heads, scale=float(hk) ** -0.5)
    grid_spec = pltpu.PrefetchScalarGridSpec(
        num_scalar_prefetch=1,
        grid=(db, 1 + n_pages // p),
        in_specs=(
            [pl.BlockSpec((1, n_cols), lambda b, s, pt: (0, 0)),
             pl.BlockSpec((None, qk, n_cols), lambda b, s, pt: (b, 0, 0)),
             pl.BlockSpec((None, page, qk), lambda b, s, pt: (b, 0, 0)),
             pl.BlockSpec((None, page, br), lambda b, s, pt: (b, 0, 0))]
            + [pl.BlockSpec((None, rows, hk), page_map(i)) for i in range(p)]
            + [pl.BlockSpec((None, rows, hv), page_map(i)) for i in range(p)]
            + [pl.BlockSpec((None, t_new, br), lambda b, s, pt: (b, 0, 0))]),
        out_specs=pl.BlockSpec((None, t_new, br), lambda b, s, pt: (b, 0, 0)),
        scratch_shapes=[pltpu.VMEM((n_cols, hv), F32), pltpu.VMEM((1, n_cols), F32),
                        pltpu.VMEM((page, n_cols), F32)],
    )
    return pl.pallas_call(
        kern,
        out_shape=jax.ShapeDtypeStruct((db, t_new, br), BF16),
        grid_spec=grid_spec,
        compiler_params=pltpu.CompilerParams(
            dimension_semantics=("arbitrary", "arbitrary"), vmem_limit_bytes=_vmem(48)),
        name="sb_decode",
    )(page_table, bias_row, wq, k_new, v_new, *([ck] * p), *([cv] * p), g3)


def kernel(x_prompt, x_sample, state_hgrn, cache_k, cache_v, page_table, c_prompt, c_sample, w_ada, b_ada,
           w_in, w_out, hgrn_lb_logits, hgrn_norm_gain, sb_logit_bias, ln_gain, ln_bias):
    b, t, d = x_prompt.shape
    db, dt, _ = x_sample.shape
    depth = w_in.shape[0]
    _, _, heads, hk, hv = state_hgrn.shape
    qk = heads * hk
    br = heads * hv
    in_width = w_in.shape[-1]
    alpha = (2.0 * depth) ** 0.25

    n_c = b + db
    c_all = jnp.pad(jnp.concatenate([c_prompt, c_sample], axis=0), ((0, (-n_c) % 8), (0, 0)))
    mods = _ada_call(c_all, w_ada, b_ada)
    w_in_bf = w_in.astype(BF16)
    w_out_bf = w_out.astype(BF16)

    tm_p = min(t, 1024)
    tm_o = min(t, 512)
    xp = x_prompt.reshape(b * t, d)
    xs = x_sample.reshape(db * dt, d)
    t_pad = -(-dt // HGRN_CHUNK) * HGRN_CHUNK
    hgrn_p, hgrn_s, kp_rows, vp_rows, ks_rows, vs_rows = [], [], [], [], [], []
    for layer in range(depth):
        slot = layer // N_MIXERS
        m = mods[layer]
        mp = [m[:b, i * d:(i + 1) * d].reshape(b, 1, d) for i in range(3)]
        ms = [jnp.repeat(m[b:n_c, i * d:(i + 1) * d], dt, axis=0).reshape(1, db * dt, d) for i in range(3)]
        up, us_copies = _inproj_call(xp, mp[0], mp[1], xs, ms[0][0], ms[1][0], w_in_bf, layer, t, tm_p)
        us = us_copies[0]
        up3 = up.reshape(b, t, in_width)
        us3 = us.reshape(db, dt, in_width)
        if layer % N_MIXERS == 0:
            bp, sp = _hgrn_call(up3, hgrn_lb_logits, hgrn_norm_gain, jnp.zeros((b, heads, hk, hv), F32),
                                slot, heads, hk, hv)
            us_pad = jnp.pad(us3, ((0, 0), (0, t_pad - dt), (0, 0)))
            bs, ss = _hgrn_call(us_pad, hgrn_lb_logits, hgrn_norm_gain, state_hgrn[slot],
                                slot, heads, hk, hv, t_valid=dt)
            bs = bs[:, :dt]
            hgrn_p.append(sp.astype(state_hgrn.dtype))
            hgrn_s.append(ss.astype(state_hgrn.dtype))
        else:
            bias = sb_logit_bias[slot].astype(F32)
            bp, kp_flat, vp_flat = _sbp_call(up3, bias, heads, hk, hv, (cache_k.dtype, cache_v.dtype))
            bs = _sbs_call(us3, cache_k, cache_v, page_table, bias, slot, heads, hk, hv)
            kp_rows.append(kp_flat.reshape(b, t, heads, hk))
            vp_rows.append(vp_flat.reshape(b, t, heads, hv))
            ks_rows.append(us3[..., qk:2 * qk].reshape(db, dt, heads, hk).astype(cache_k.dtype))
            vs_rows.append(us3[..., 2 * qk:2 * qk + br].reshape(db, dt, heads, hv).astype(cache_v.dtype))
        xp = _outproj_call(xp, bp.reshape(b * t, br), w_out_bf, mp[2], ln_gain, ln_bias, layer, t, tm_o, alpha)
        xs = _outproj_call(xs, bs.reshape(db * dt, br), w_out_bf, ms[2], ln_gain, ln_bias, layer,
                           db * dt, db * dt, alpha)
    return (xp.reshape(b, t, d), xs.reshape(db, dt, d),
            jnp.stack(hgrn_p, axis=0), jnp.stack(hgrn_s, axis=0),
            jnp.stack(kp_rows, axis=2), jnp.stack(vp_rows, axis=2),
            jnp.stack(ks_rows, axis=2), jnp.stack(vs_rows, axis=2))
```

```python
import functools

import jax
import jax.numpy as jnp
from jax import lax
from jax.experimental import pallas as pl
from jax.experimental.pallas import tpu as pltpu

F32 = jnp.float32
BF16 = jnp.bfloat16

LN_EPS = 1e-5
RMS_EPS = 1e-6
HGRN_CHUNK = 64
HGRN_TBLK = 128
HGRN_HEADS_PER_STEP = 16
SBP_TQ = 1024
SBP_TK = 128
N_MIXERS = 2

NT_DIMS = (((1,), (1,)), ((), ()))
TN_DIMS = (((0,), (0,)), ((), ()))


def _vmem(mib):
    return mib * 1024 * 1024


def _sigmoid(x):
    return 1.0 / (1.0 + jnp.exp(-x))


def _silu(x):
    return x * _sigmoid(x)


def _softplus(z):
    return jnp.maximum(z, 0.0) + jnp.log(1.0 + jnp.exp(-jnp.abs(z)))


def _split2(x):
    hi = x.astype(BF16)
    lo = (x - hi.astype(F32)).astype(BF16)
    return hi, lo


def _split3(x):
    hi = x.astype(BF16)
    r = x - hi.astype(F32)
    mid = r.astype(BF16)
    lo = (r - mid.astype(F32)).astype(BF16)
    return hi, mid, lo


def _ada_kernel(c_ref, w_ref, b_ref, o_ref):
    o_ref[...] = jnp.dot(c_ref[...], w_ref[...], precision=lax.Precision.HIGHEST,
                         preferred_element_type=F32) + b_ref[...]


def _ada_call(c_all, w_ada, b_ada):
    depth, d, n = w_ada.shape
    rows = c_all.shape[0]
    tn = 2048
    return pl.pallas_call(
        _ada_kernel,
        out_shape=jax.ShapeDtypeStruct((depth, rows, n), F32),
        grid=(depth, n // tn),
        in_specs=[
            pl.BlockSpec((rows, d), lambda l, j: (0, 0)),
            pl.BlockSpec((None, d, tn), lambda l, j: (l, 0, j)),
            pl.BlockSpec((None, 1, tn), lambda l, j: (l, 0, j)),
        ],
        out_specs=pl.BlockSpec((None, rows, tn), lambda l, j: (l, 0, j)),
        compiler_params=pltpu.CompilerParams(
            dimension_semantics=("arbitrary", "arbitrary"), vmem_limit_bytes=_vmem(40)),
        name="ada_mod",
    )(c_all, w_ada, b_ada.reshape(depth, 1, n))


def _inproj_kernel(x_ref, shift_ref, scale_ref, xs_ref, shift_s_ref, scale_s_ref, w_ref, u_ref, us_ref, h_ref,
                   *, tm):
    @pl.when(pl.program_id(1) == 0)
    def _():
        h_ref[:tm, :] = (x_ref[...] * (1.0 + scale_ref[...]) + shift_ref[...]).astype(BF16)
        h_ref[tm:, :] = (xs_ref[...] * (1.0 + scale_s_ref[...]) + shift_s_ref[...]).astype(BF16)

    u = jnp.dot(h_ref[...], w_ref[...], preferred_element_type=F32)
    u_ref[...] = u[:tm]
    us_ref[...] = u[tm:]


def _inproj_call(x2, shift, scale, xs2, shift_s, scale_s, w_bf, layer, rows_per_mod, tm):
    m, d = x2.shape
    ms = xs2.shape[0]
    n = w_bf.shape[-1]
    tn = 1024
    tiles_per_mod = rows_per_mod // tm
    mod_spec = pl.BlockSpec((None, 1, d), lambda i, j: (i // tiles_per_mod, 0, 0))
    s_spec = pl.BlockSpec((ms, d), lambda i, j: (0, 0))
    return pl.pallas_call(
        functools.partial(_inproj_kernel, tm=tm),
        out_shape=(jax.ShapeDtypeStruct((m, n), F32), jax.ShapeDtypeStruct((m // tm, ms, n), F32)),
        grid=(m // tm, n // tn),
        in_specs=[
            pl.BlockSpec((tm, d), lambda i, j: (i, 0)),
            mod_spec, mod_spec,
            s_spec, s_spec, s_spec,
            pl.BlockSpec((None, d, tn), lambda i, j: (layer, 0, j)),
        ],
        out_specs=(pl.BlockSpec((tm, tn), lambda i, j: (i, j)),
                   pl.BlockSpec((None, ms, tn), lambda i, j: (i, 0, j))),
        scratch_shapes=[pltpu.VMEM((tm + ms, d), BF16)],
        compiler_params=pltpu.CompilerParams(
            dimension_semantics=("arbitrary", "arbitrary"), vmem_limit_bytes=_vmem(48)),
        name="in_proj",
    )(x2, shift, scale, xs2, shift_s, scale_s, w_bf)


def _outproj_kernel(x_ref, br_ref, w_ref, gate_ref, lng_ref, lnb_ref, y_ref, *, alpha):
    v = alpha * x_ref[...] + gate_ref[...] * jnp.dot(br_ref[...], w_ref[...], preferred_element_type=F32)
    mu = jnp.mean(v, axis=-1, keepdims=True)
    dv = v - mu
    var = jnp.mean(dv * dv, axis=-1, keepdims=True)
    y_ref[...] = dv * lax.rsqrt(var + LN_EPS) * lng_ref[...] + lnb_ref[...]


def _outproj_call(x2, br2, w_bf, gate, ln_gain, ln_bias, layer, rows_per_mod, tm, alpha):
    m, d = x2.shape
    kdim = br2.shape[1]
    r = gate.shape[1]
    tiles_per_mod = rows_per_mod // tm
    depth = ln_gain.shape[0]
    vec_spec = pl.BlockSpec((None, 1, d), lambda i: (layer, 0, 0))
    return pl.pallas_call(
        functools.partial(_outproj_kernel, alpha=alpha),
        out_shape=jax.ShapeDtypeStruct((m, d), F32),
        grid=(m // tm,),
        in_specs=[
            pl.BlockSpec((tm, d), lambda i: (i, 0)),
            pl.BlockSpec((tm, kdim), lambda i: (i, 0)),
            pl.BlockSpec((None, kdim, d), lambda i: (layer, 0, 0), pipeline_mode=pl.Buffered(1)),
            pl.BlockSpec((None, r, d), lambda i: (i // tiles_per_mod, 0, 0)),
            vec_spec, vec_spec,
        ],
        out_specs=pl.BlockSpec((tm, d), lambda i: (i, 0)),
        compiler_params=pltpu.CompilerParams(
            dimension_semantics=("arbitrary",), vmem_limit_bytes=_vmem(52)),
        name="out_proj_ln",
    )(x2, br2, w_bf, gate, ln_gain.reshape(depth, 1, d), ln_bias.reshape(depth, 1, d))


def _hgrn_kernel(lbl_ref, gain_ref, q_ref, f_ref, i_ref, g_ref, s0_ref, o_ref, s_ref, st_ref, *pad_refs,
                 slot, chunk, n_chunks, n_heads, hk, hv, t_valid):
    tb = pl.program_id(2)
    c = chunk
    if t_valid is not None:
        for src, dst in zip((q_ref, f_ref, i_ref, g_ref), pad_refs):
            dst[...] = jnp.zeros_like(dst)
            dst[0:t_valid, :] = src[...]
        q_ref, f_ref, i_ref, g_ref = pad_refs

    @pl.when(tb == 0)
    def _():
        for hh in range(n_heads):
            st_ref[hh] = s0_ref[hh].T

    lg = lbl_ref[...]
    e = jnp.exp(lg - jnp.max(lg, axis=0, keepdims=True))
    lb_all = jnp.sum(e[:slot + 1], axis=0, keepdims=True) / jnp.sum(e, axis=0, keepdims=True)
    gain = gain_ref[...]

    row = lax.broadcasted_iota(jnp.int32, (c, c), 0)
    col = lax.broadcasted_iota(jnp.int32, (c, c), 1)
    causal = col <= row
    tri = jnp.where(causal, 1.0, 0.0).astype(BF16)
    tri3 = jnp.concatenate([tri, tri, tri], axis=1)

    gain_all = jnp.concatenate([gain] * n_heads, axis=1)
    for ci in range(n_chunks):
        sl = pl.ds(ci * c, c)
        zf = f_ref[sl, :]
        t = jnp.exp(-jnp.abs(zf))
        r = 1.0 / (1.0 + t)
        pos = zf >= 0.0
        sig = jnp.where(pos, r, t * r)
        nsig = jnp.where(pos, t * r, r)
        log_f = jnp.log(lb_all + (1.0 - lb_all) * sig)
        k = (1.0 - lb_all) * nsig
        if t_valid is not None:
            tpos = (tb * n_chunks + ci) * c + lax.broadcasted_iota(jnp.int32, (c, 1), 0)
            valid = tpos < t_valid
            log_f = jnp.where(valid, log_f, 0.0)
            k = jnp.where(valid, k, 0.0)
        cum = jnp.dot(tri3, jnp.concatenate(_split3(log_f), axis=0),
                      preferred_element_type=F32)
        ref = cum[c // 2:c // 2 + 1, :]
        last = cum[c - 1:c, :]
        q = _silu(q_ref[sl, :])
        qe = (q * jnp.exp(cum - ref)).astype(BF16)
        ke = (k * jnp.exp(ref - cum)).astype(BF16)
        qd = (q * jnp.exp(cum)).astype(BF16)
        kd = (k * jnp.exp(last - cum)).astype(BF16)
        decay = jnp.exp(last)
        vb_all = i_ref[sl, :].astype(BF16)
        normed = []
        for hh in range(n_heads):
            ks = slice(hh * hk, (hh + 1) * hk)
            vb = vb_all[:, hh * hv:(hh + 1) * hv]
            scores = lax.dot_general(qe[:, ks], ke[:, ks], NT_DIMS, preferred_element_type=F32)
            scores = jnp.where(causal, scores, 0.0).astype(BF16)
            st = st_ref[hh]
            o = (jnp.dot(scores, vb, preferred_element_type=F32)
                 + lax.dot_general(qd[:, ks], st.astype(BF16), NT_DIMS, preferred_element_type=F32))
            st_ref[hh] = decay[:, ks] * st + lax.dot_general(vb, kd[:, ks], TN_DIMS,
                                                            preferred_element_type=F32)
            ms = jnp.mean(o * o, axis=-1, keepdims=True)
            normed.append(o * lax.rsqrt(ms + RMS_EPS))
        on = jnp.concatenate(normed, axis=1)
        o_ref[sl, :] = (on * gain_all * _silu(g_ref[sl, :])).astype(o_ref.dtype)

    @pl.when(tb == pl.num_programs(2) - 1)
    def _():
        for hh in range(n_heads):
            s_ref[hh] = st_ref[hh].T


def _hgrn_call(u3, lb_logits, norm_gain, s0, slot, heads, hk, hv, t_valid=None):
    b, t, _ = u3.shape
    if t_valid is not None:
        assert t == t_valid
        t = -(-t_valid // HGRN_CHUNK) * HGRN_CHUNK
        assert t <= HGRN_TBLK
    tblk = min(t, HGRN_TBLK)
    rows_in = tblk if t_valid is None else t_valid
    n_chunks = tblk // HGRN_CHUNK
    hb = HGRN_HEADS_PER_STEP
    groups = heads // hb
    qk = heads * hk
    v_off = 2 * qk // (hb * hv)
    n_lb = lb_logits.shape[0]
    kern = functools.partial(_hgrn_kernel, slot=slot, chunk=HGRN_CHUNK, n_chunks=n_chunks,
                             n_heads=hb, hk=hk, hv=hv, t_valid=t_valid)
    return pl.pallas_call(
        kern,
        out_shape=(jax.ShapeDtypeStruct((b, t, heads * hv), BF16),
                   jax.ShapeDtypeStruct((b, heads, hk, hv), F32)),
        grid=(b, groups, t // tblk),
        in_specs=[
            pl.BlockSpec((n_lb, hb * hk), lambda bi, g, tb: (0, g)),
            pl.BlockSpec((None, 1, hv), lambda bi, g, tb: (slot, 0, 0)),
            pl.BlockSpec((None, rows_in, hb * hk), lambda bi, g, tb: (bi, tb, g)),
            pl.BlockSpec((None, rows_in, hb * hk), lambda bi, g, tb: (bi, tb, groups + g)),
            pl.BlockSpec((None, rows_in, hb * hv), lambda bi, g, tb: (bi, tb, v_off + g)),
            pl.BlockSpec((None, rows_in, hb * hv), lambda bi, g, tb: (bi, tb, v_off + groups + g)),
            pl.BlockSpec((None, hb, hk, hv), lambda bi, g, tb: (bi, g, 0, 0)),
        ],
        out_specs=(
            pl.BlockSpec((None, tblk, hb * hv), lambda bi, g, tb: (bi, tb, g)),
            pl.BlockSpec((None, hb, hk, hv), lambda bi, g, tb: (bi, g, 0, 0)),
        ),
        scratch_shapes=[pltpu.VMEM((hb, hv, hk), F32)] + ([] if t_valid is None else [
            pltpu.VMEM((tblk, hb * hk), F32), pltpu.VMEM((tblk, hb * hk), F32),
            pltpu.VMEM((tblk, hb * hv), F32), pltpu.VMEM((tblk, hb * hv), F32)]),
        compiler_params=pltpu.CompilerParams(
            dimension_semantics=("arbitrary", "arbitrary", "arbitrary"), vmem_limit_bytes=_vmem(40)),
        name="hgrn_mixer",
    )(lb_logits, norm_gain.reshape(-1, 1, hv), u3, u3, u3, u3, s0)


def _sbp_kernel(bias_ref, q_ref, k_ref, v_ref, g_ref, o_ref, ko_ref, vo_ref, acc_ref, *, tq, tk, scale):
    h = pl.program_id(1)
    qi = pl.program_id(2)

    @pl.when(qi == 0)
    def _():
        ko_ref[...] = k_ref[...].astype(ko_ref.dtype)
        vo_ref[...] = v_ref[...].astype(vo_ref.dtype)

    bias = bias_ref[h]
    q = (q_ref[...] * scale).astype(BF16)
    row = lax.broadcasted_iota(jnp.int32, (tk, tk), 0)
    col = lax.broadcasted_iota(jnp.int32, (tk, tk), 1)
    suffix = jnp.where(row >= col, 1.0, 0.0).astype(BF16)
    suffix2 = jnp.concatenate([suffix, suffix], axis=0)
    acc_ref[...] = jnp.zeros_like(acc_ref)

    def pair(start, carry, r0):
        masked = r0 is not None
        r0 = r0 or 0
        rows = tq - r0
        kb = k_ref[pl.ds(start, 2 * tk), :].astype(BF16)
        vb = v_ref[pl.ds(start, 2 * tk), :].astype(BF16)
        z = lax.dot_general(q[r0:], kb, NT_DIMS, preferred_element_type=F32) + bias
        sp = _softplus(z)
        if masked:
            q_pos = qi * tq + r0 + lax.broadcasted_iota(jnp.int32, (rows, 2 * tk), 0)
            k_pos = start + lax.broadcasted_iota(jnp.int32, (rows, 2 * tk), 1)
            keep = k_pos < q_pos
            sp = jnp.where(keep, sp, 0.0)
        hi, lo = _split2(sp)
        rs_new = jnp.dot(jnp.concatenate([hi[:, tk:], lo[:, tk:]], axis=1), suffix2,
                         preferred_element_type=F32)
        rs_old = jnp.dot(jnp.concatenate([hi[:, :tk], lo[:, :tk]], axis=1), suffix2,
                         preferred_element_type=F32)
        carry_old = carry + rs_new[:, 0:1]
        w = jnp.concatenate([jnp.exp(z[:, :tk] - rs_old - carry_old),
                             jnp.exp(z[:, tk:] - rs_new - carry)], axis=1)
        if masked:
            w = jnp.where(keep, w, 0.0)
        acc_ref[r0:, :] += jnp.dot(w.astype(BF16), vb, preferred_element_type=F32)
        return carry_old + rs_old[:, 0:1]

    span = 2 * tk
    n_diag = tq // span
    carry = jnp.zeros((tq, 1), F32)
    for dj in range(n_diag):
        r0 = (n_diag - 1 - dj) * span
        c_new = pair(pl.multiple_of(qi * tq + r0, span), carry[r0:], r0)
        carry = c_new if r0 == 0 else jnp.concatenate([carry[:r0], c_new], axis=0)

    def body(jj, c):
        for dj in range(n_diag):
            c = pair(pl.multiple_of(qi * tq - (jj * n_diag + dj + 1) * span, span), c, None)
        return c

    lax.fori_loop(0, qi, body, carry)
    o_ref[...] = (acc_ref[...] * _silu(g_ref[...])).astype(o_ref.dtype)


def _sbp_call(u3, bias, heads, hk, hv, row_dtypes):
    b, t, _ = u3.shape
    tq = min(t, SBP_TQ)
    tk = min(tq // 2, SBP_TK)
    qk = heads * hk
    kv = qk // hv
    kern = functools.partial(_sbp_kernel, tq=tq, tk=tk, scale=float(hk) ** -0.5)
    grid_spec = pltpu.PrefetchScalarGridSpec(
        num_scalar_prefetch=1,
        grid=(b, heads, t // tq),
        in_specs=[
            pl.BlockSpec((None, tq, hk), lambda bi, h, qi, bias: (bi, qi, h)),
            pl.BlockSpec((None, t, hk), lambda bi, h, qi, bias: (bi, 0, heads + h)),
            pl.BlockSpec((None, t, hv), lambda bi, h, qi, bias: (bi, 0, 2 * kv + h)),
            pl.BlockSpec((None, tq, hv), lambda bi, h, qi, bias: (bi, qi, 2 * kv + heads + h)),
        ],
        out_specs=(
            pl.BlockSpec((None, tq, hv), lambda bi, h, qi, bias: (bi, qi, h)),
            pl.BlockSpec((None, t, hk), lambda bi, h, qi, bias: (bi, 0, h)),
            pl.BlockSpec((None, t, hv), lambda bi, h, qi, bias: (bi, 0, h)),
        ),
        scratch_shapes=[pltpu.VMEM((tq, hv), F32)],
    )
    return pl.pallas_call(
        kern,
        out_shape=(jax.ShapeDtypeStruct((b, t, heads * hv), BF16),
                   jax.ShapeDtypeStruct((b, t, qk), row_dtypes[0]),
                   jax.ShapeDtypeStruct((b, t, heads * hv), row_dtypes[1])),
        grid_spec=grid_spec,
        compiler_params=pltpu.CompilerParams(
            dimension_semantics=("arbitrary", "arbitrary", "arbitrary"), vmem_limit_bytes=_vmem(56)),
        name="sb_prompt",
    )(bias, u3, u3, u3, u3)


def _sbs_kernel(pt_ref, bias_ref, wq_ref, kn_ref, vn_ref, *rest, pages_per_step, t_new, heads, hv,
                row_stride, row_start, scale):
    p = pages_per_step
    k_refs = rest[:p]
    v_refs = rest[p:2 * p]
    g_ref, o_ref, acc_ref, carry_ref, wbuf_ref, knp_ref, vnp_ref = rest[2 * p:]
    s = pl.program_id(1)
    n_keys = knp_ref.shape[0]
    n_cols = wq_ref.shape[1]

    row = lax.broadcasted_iota(jnp.int32, (n_keys, n_keys), 0)
    col = lax.broadcasted_iota(jnp.int32, (n_keys, n_keys), 1)
    newer = jnp.where(col > row, 1.0, 0.0).astype(BF16)
    newer2 = jnp.concatenate([newer, newer], axis=1)
    hsel_h = lax.broadcasted_iota(jnp.int32, (row_stride, n_cols), 0) - row_start
    hsel_c = lax.broadcasted_iota(jnp.int32, (row_stride, n_cols), 1)
    head_sel = jnp.where(jnp.logical_and(hsel_c // t_new == hsel_h, hsel_c < heads * t_new), 1.0, 0.0)

    def weights(kflat, mask):
        zt = jnp.dot(kflat, wq_ref[...], preferred_element_type=F32) * scale + bias_ref[...]
        sp = _softplus(zt)
        if mask is not None:
            sp = jnp.where(mask, sp, 0.0)
        hi, lo = _split2(sp)
        later = jnp.dot(newer2, jnp.concatenate([hi, lo], axis=0), preferred_element_type=F32)
        w = jnp.exp(zt - sp - later - carry_ref[...])
        if mask is not None:
            w = jnp.where(mask, w, 0.0)
        carry_ref[...] += later[0:1, :] + sp[0:1, :]
        return w

    def head_rows(ref, h):
        return ref[pl.ds(row_start + h, n_keys, stride=row_stride), :]

    @pl.when(s == 0)
    def _():
        acc_ref[...] = jnp.zeros_like(acc_ref)
        carry_ref[...] = jnp.zeros_like(carry_ref)
        key_pos = lax.broadcasted_iota(jnp.int32, (n_keys, n_cols), 0)
        q_pos = lax.broadcasted_iota(jnp.int32, (n_keys, n_cols), 1) % t_new
        knp_ref[...] = jnp.zeros_like(knp_ref)
        vnp_ref[...] = jnp.zeros_like(vnp_ref)
        knp_ref[0:t_new, :] = kn_ref[...]
        vnp_ref[0:t_new, :] = vn_ref[...]
        w = weights(knp_ref[...].astype(BF16), key_pos < q_pos)
        for h in range(heads):
            wh = (w * head_sel[row_start + h:row_start + h + 1, :]).astype(BF16)
            acc_ref[...] += lax.dot_general(wh, vnp_ref[:, h * hv:(h + 1) * hv].astype(BF16), TN_DIMS,
                                            preferred_element_type=F32)

    @pl.when(s > 0)
    def _():
        for i in range(p):
            kflat = jnp.concatenate([head_rows(k_refs[i], h).astype(BF16) for h in range(heads)], axis=1)
            wbuf_ref[...] = weights(kflat, None)
            wexp = jnp.concatenate(
                [jnp.broadcast_to(wbuf_ref[key:key + 1, :], (row_stride, n_cols)) * head_sel
                 for key in range(n_keys)], axis=0).astype(BF16)
            acc_ref[...] += lax.dot_general(wexp, v_refs[i][...].astype(BF16), TN_DIMS,
                                            preferred_element_type=F32)

    @pl.when(s == pl.num_programs(1) - 1)
    def _():
        for h in range(heads):
            o = acc_ref[h * t_new:(h + 1) * t_new, :]
            o_ref[:, h * hv:(h + 1) * hv] = (o * _silu(g_ref[:, h * hv:(h + 1) * hv])).astype(o_ref.dtype)


def _sbs_call(us3, cache_k, cache_v, page_table, bias, slot, heads, hk, hv):
    db, t_new, _ = us3.shape
    n_pool, page, n_b = cache_k.shape[:3]
    n_pages = page_table.shape[1]
    qk = heads * hk
    br = heads * hv
    n_cols = -(-heads * t_new // 128) * 128
    p = 4
    while n_pages % p:
        p //= 2
    q = us3[..., :qk].reshape(db, t_new, heads, hk)
    eye = jnp.eye(heads, n_cols // t_new, dtype=F32)
    wq = (q.transpose(0, 2, 3, 1)[:, :, :, None, :] * eye[None, :, None, :, None])
    wq = wq.reshape(db, qk, n_cols).astype(BF16)
    k_new = us3[..., qk:2 * qk]
    v_new = us3[..., 2 * qk:2 * qk + br]
    g3 = us3[..., 2 * qk + br:]
    bias_row = jnp.pad(jnp.repeat(bias.astype(F32), t_new), (0, n_cols - heads * t_new)).reshape(1, n_cols)
    rows = page * n_b * heads
    ck = cache_k.reshape(n_pool, rows, hk)
    cv = cache_v.reshape(n_pool, rows, hv)

    def page_map(i):
        def index_map(b, s, pt):
            logical = n_pages - 1 - (jnp.maximum(s - 1, 0) * p + i)
            return (pt[b, logical], 0, 0)
        return index_map

    kern = functools.partial(_sbs_kernel, pages_per_step=p, t_new=t_new, heads=heads, hv=hv,
                             row_stride=n_b * heads, row_start=slot * heads, scale=float(hk) ** -0.5)
    grid_spec = pltpu.PrefetchScalarGridSpec(
        num_scalar_prefetch=1,
        grid=(db, 1 + n_pages // p),
        in_specs=(
            [pl.BlockSpec((1, n_cols), lambda b, s, pt: (0, 0)),
             pl.BlockSpec((None, qk, n_cols), lambda b, s, pt: (b, 0, 0)),
             pl.BlockSpec((None, t_new, qk), lambda b, s, pt: (b, 0, 0)),
             pl.BlockSpec((None, t_new, br), lambda b, s, pt: (b, 0, 0))]
            + [pl.BlockSpec((None, rows, hk), page_map(i)) for i in range(p)]
            + [pl.BlockSpec((None, rows, hv), page_map(i)) for i in range(p)]
            + [pl.BlockSpec((None, t_new, br), lambda b, s, pt: (b, 0, 0))]),
        out_specs=pl.BlockSpec((None, t_new, br), lambda b, s, pt: (b, 0, 0)),
        scratch_shapes=[pltpu.VMEM((n_cols, hv), F32), pltpu.VMEM((1, n_cols), F32),
                        pltpu.VMEM((page, n_cols), F32),
                        pltpu.VMEM((page, qk), F32), pltpu.VMEM((page, br), F32)],
    )
    return pl.pallas_call(
        kern,
        out_shape=jax.ShapeDtypeStruct((db, t_new, br), BF16),
        grid_spec=grid_spec,
        compiler_params=pltpu.CompilerParams(
            dimension_semantics=("arbitrary", "arbitrary"), vmem_limit_bytes=_vmem(48)),
        name="sb_decode",
    )(page_table, bias_row, wq, k_new, v_new, *([ck] * p), *([cv] * p), g3)


def kernel(x_prompt, x_sample, state_hgrn, cache_k, cache_v, page_table, c_prompt, c_sample, w_ada, b_ada,
           w_in, w_out, hgrn_lb_logits, hgrn_norm_gain, sb_logit_bias, ln_gain, ln_bias):
    b, t, d = x_prompt.shape
    db, dt, _ = x_sample.shape
    depth = w_in.shape[0]
    _, _, heads, hk, hv = state_hgrn.shape
    qk = heads * hk
    br = heads * hv
    in_width = w_in.shape[-1]
    alpha = (2.0 * depth) ** 0.25

    n_c = b + db
    c_all = jnp.pad(jnp.concatenate([c_prompt, c_sample], axis=0), ((0, (-n_c) % 8), (0, 0)))
    mods = _ada_call(c_all, w_ada, b_ada)
    w_in_bf = w_in.astype(BF16)
    w_out_bf = w_out.astype(BF16)

    tm_p = min(t, 1024)
    tm_o = min(t, 512)
    xp = x_prompt.reshape(b * t, d)
    xs = x_sample.reshape(db * dt, d)
    hgrn_p, hgrn_s, kp_rows, vp_rows, ks_rows, vs_rows = [], [], [], [], [], []
    for layer in range(depth):
        slot = layer // N_MIXERS
        m = mods[layer]
        mp = [m[:b, i * d:(i + 1) * d].reshape(b, 1, d) for i in range(3)]
        ms = [jnp.repeat(m[b:n_c, i * d:(i + 1) * d], dt, axis=0).reshape(1, db * dt, d) for i in range(3)]
        up, us_copies = _inproj_call(xp, mp[0], mp[1], xs, ms[0][0], ms[1][0], w_in_bf, layer, t, tm_p)
        us = us_copies[0]
        up3 = up.reshape(b, t, in_width)
        us3 = us.reshape(db, dt, in_width)
        if layer % N_MIXERS == 0:
            bp, sp = _hgrn_call(up3, hgrn_lb_logits, hgrn_norm_gain, jnp.zeros((b, heads, hk, hv), F32),
                                slot, heads, hk, hv)
            bs, ss = _hgrn_call(us3, hgrn_lb_logits, hgrn_norm_gain, state_hgrn[slot],
                                slot, heads, hk, hv, t_valid=dt)
            bs = bs[:, :dt]
            hgrn_p.append(sp.astype(state_hgrn.dtype))
            hgrn_s.append(ss.astype(state_hgrn.dtype))
        else:
            bias = sb_logit_bias[slot].astype(F32)
            bp, kp_flat, vp_flat = _sbp_call(up3, bias, heads, hk, hv, (cache_k.dtype, cache_v.dtype))
            bs = _sbs_call(us3, cache_k, cache_v, page_table, bias, slot, heads, hk, hv)
            kp_rows.append(kp_flat.reshape(b, t, heads, hk))
            vp_rows.append(vp_flat.reshape(b, t, heads, hv))
            ks_rows.append(us3[..., qk:2 * qk].reshape(db, dt, heads, hk).astype(cache_k.dtype))
            vs_rows.append(us3[..., 2 * qk:2 * qk + br].reshape(db, dt, heads, hv).astype(cache_v.dtype))
        xp = _outproj_call(xp, bp.reshape(b * t, br), w_out_bf, mp[2], ln_gain, ln_bias, layer, t, tm_o, alpha)
        xs = _outproj_call(xs, bs.reshape(db * dt, br), w_out_bf, ms[2], ln_gain, ln_bias, layer,
                           db * dt, db * dt, alpha)
    return (xp.reshape(b, t, d), xs.reshape(db, dt, d),
            jnp.stack(hgrn_p, axis=0), jnp.stack(hgrn_s, axis=0),
            jnp.stack(kp_rows, axis=2), jnp.stack(vp_rows, axis=2),
            jnp.stack(ks_rows, axis=2), jnp.stack(vs_rows, axis=2))
```
